```python
import jax, jax.numpy as jnp
from jax import lax
import numpy as np

D_MODEL = 4096
BATCH = 1
SEQ = 16384
DEPTH = 1
DEC_BATCH = 16
DEC_SEQ = 16
PAST_LEN = 2048

CHUNK = 64
N_META = 16
HEAD_DIM = 128
D_MIX = D_MODEL
D_ATTN = D_MIX // 2
D_RNN = D_MIX - D_ATTN
N_HEADS = D_ATTN // HEAD_DIM
N_KV = N_HEADS // 4
GROUP = N_HEADS // N_KV
N_RNN_BLOCKS = 16
RNN_BLOCK = D_RNN // N_RNN_BLOCKS
RNN_CONV_W = 4
LRU_C = 8.0
D_FF = 3 * D_MODEL
FFN_CONV_W = 3
Q_BLOCK = 128
EPS = 1e-6
FORGET_BIAS = 3.0

COL_X = 0
COL_Y = COL_X + D_RNN
COL_Q = COL_Y + D_RNN
COL_K = COL_Q + D_ATTN
COL_V = COL_K + N_KV * HEAD_DIM
COL_F = COL_V + N_KV * HEAD_DIM
IN_COLS = COL_F + N_HEADS

kernel_name = "hymba_rglru_fox_convffn_stream_step"


def rms_norm(x, g):
    xf = x.astype(jnp.float32)
    y = xf * lax.rsqrt(jnp.mean(xf * xf, axis=-1, keepdims=True) + EPS) * g.astype(jnp.float32)
    return y.astype(x.dtype)


def causal_dwconv(x, buf, w, b):
    width = w.shape[0]
    T = x.shape[1]
    xp = jnp.concatenate([buf.astype(x.dtype), x], axis=1)
    out = b
    for k in range(width):
        out = out + w[k] * xp[:, k:k + T]
    return out.astype(x.dtype), xp[:, xp.shape[1] - (width - 1):]


def rg_lru(x, h0, w_a, b_a, w_x, b_x, lam):
    B, T, _ = x.shape
    xf = x.astype(jnp.float32)
    xb = xf.reshape(B, T, N_RNN_BLOCKS, RNN_BLOCK)
    r = jax.nn.sigmoid(jnp.einsum('btnc,ncd->btnd', xb, w_a.astype(jnp.float32))
                       + b_a.astype(jnp.float32).reshape(N_RNN_BLOCKS, RNN_BLOCK)).reshape(B, T, D_RNN)
    i = jax.nn.sigmoid(jnp.einsum('btnc,ncd->btnd', xb, w_x.astype(jnp.float32))
                       + b_x.astype(jnp.float32).reshape(N_RNN_BLOCKS, RNN_BLOCK)).reshape(B, T, D_RNN)
    log_a = -LRU_C * r * jax.nn.softplus(-lam.astype(jnp.float32))
    a = jnp.exp(log_a)
    bterm = jnp.sqrt(-jnp.expm1(2.0 * log_a)) * (i * xf)
    bterm = bterm.at[:, 0].add(a[:, 0] * h0.astype(jnp.float32))

    def combine(left, right):
        a1, b1 = left
        a2, b2 = right
        return a1 * a2, a2 * b1 + b2

    _, h = lax.associative_scan(combine, (a, bterm), axis=1)
    return h.astype(x.dtype), h[:, -1]


def fox_block(qg, cq, qpos, kf, ckT, vf):
    s = jnp.einsum('bqgrd,bsgd->bgrqs', qg, kf) * (HEAD_DIM ** -0.5)
    bias = cq.transpose(0, 2, 3, 1)[..., :, None] - ckT[..., None, :]
    mask = jnp.arange(kf.shape[1])[None, :] <= qpos[:, None]
    p = jax.nn.softmax(jnp.where(mask, s + bias, -jnp.inf), axis=-1)
    return jnp.einsum('bgrqs,bsgd->bqgrd', p, vf)


def fox_attention_prompt(q, k, v, logf):
    B, T = q.shape[:2]
    n_blk = -(-T // Q_BLOCK)
    pad = n_blk * Q_BLOCK - T
    c = jnp.cumsum(logf, axis=1).reshape(B, T, N_KV, GROUP)
    ckT = c.transpose(0, 2, 3, 1)
    kf = k.astype(jnp.float32)
    vf = v.astype(jnp.float32)
    qb = jnp.pad(q.astype(jnp.float32).reshape(B, T, N_KV, GROUP, HEAD_DIM),
                 ((0, 0), (0, pad), (0, 0), (0, 0), (0, 0)))
    qb = qb.reshape(B, n_blk, Q_BLOCK, N_KV, GROUP, HEAD_DIM).swapaxes(0, 1)
    cb = jnp.pad(c, ((0, 0), (0, pad), (0, 0), (0, 0))).reshape(B, n_blk, Q_BLOCK, N_KV, GROUP).swapaxes(0, 1)
    starts = jnp.arange(n_blk, dtype=jnp.int32) * Q_BLOCK

    def one(args):
        qi, ci, st = args
        return fox_block(qi, ci, st + jnp.arange(Q_BLOCK, dtype=jnp.int32), kf, ckT, vf)

    out = lax.map(one, (qb, cb, starts))
    out = out.swapaxes(0, 1).reshape(B, n_blk * Q_BLOCK, N_HEADS, HEAD_DIM)[:, :T]
    return out.astype(q.dtype)


def fox_attention_sample(q, k, v, logf, cache_k, cache_v, cache_logf):
    B, S = q.shape[:2]
    P = cache_k.shape[1]
    kf = jnp.concatenate([cache_k.astype(jnp.float32), k.astype(jnp.float32)], axis=1)
    vf = jnp.concatenate([cache_v.astype(jnp.float32), v.astype(jnp.float32)], axis=1)
    c = jnp.cumsum(jnp.concatenate([cache_logf.astype(jnp.float32), logf], axis=1), axis=1)
    c = c.reshape(B, P + S, N_KV, GROUP)
    qg = q.astype(jnp.float32).reshape(B, S, N_KV, GROUP, HEAD_DIM)
    out = fox_block(qg, c[:, P:], P + jnp.arange(S, dtype=jnp.int32), kf, c.transpose(0, 2, 3, 1), vf)
    return out.reshape(B, S, N_HEADS, HEAD_DIM).astype(q.dtype)


def layer(h, lp, cache_k, cache_v, cache_logf, h0, rnn_buf, ffn_buf):
    B, T, _ = h.shape
    u = rms_norm(h, lp['g_mix']) @ lp['w_in']
    xr = u[..., COL_X:COL_Y]
    yg = u[..., COL_Y:COL_Q]
    q = u[..., COL_Q:COL_K].reshape(B, T, N_HEADS, HEAD_DIM)
    k = u[..., COL_K:COL_V].reshape(B, T, N_KV, HEAD_DIM)
    v = u[..., COL_V:COL_F].reshape(B, T, N_KV, HEAD_DIM)
    logf = jax.nn.log_sigmoid(u[..., COL_F:].astype(jnp.float32) + lp['b_f'].astype(jnp.float32))
    xc, rnn_buf_new = causal_dwconv(xr, rnn_buf, lp['w_rnn_conv'], lp['b_rnn_conv'])
    hr, h_last = rg_lru(xc, h0, lp['w_rg_a'], lp['b_rg_a'], lp['w_rg_x'], lp['b_rg_x'], lp['lru_lambda'])
    o_rnn = hr * jax.nn.gelu(yg)
    if cache_k is None:
        o_attn = fox_attention_prompt(q, k, v, logf)
    else:
        o_attn = fox_attention_sample(q, k, v, logf, cache_k, cache_v, cache_logf)
    o = jnp.concatenate([rms_norm(o_rnn, lp['g_out_rnn']),
                         rms_norm(o_attn.reshape(B, T, D_ATTN), lp['g_out_attn'])], axis=-1)
    h = h + o @ lp['w_out']
    z = rms_norm(h, lp['g_ffn']) @ lp['w_ffn_in']
    gate, ffn_buf_new = causal_dwconv(z[..., :D_FF], ffn_buf, lp['w_ffn_conv'], lp['b_ffn_conv'])
    h = h + (jax.nn.gelu(gate) * z[..., D_FF:]) @ lp['w_ffn_out']
    return h, (k, v, logf.astype(h.dtype), h_last.astype(h.dtype), rnn_buf_new, ffn_buf_new)


def setup_inputs(seed: int = 0) -> dict:
    key = jax.random.key(seed)
    ks = iter(jax.random.split(key, 40))
    f32 = jnp.float32

    def nrm(shape, scale):
        return jax.random.normal(next(ks), shape, f32) * scale

    x_prompt = nrm((BATCH, SEQ, D_MODEL), 1.0)
    x_sample = nrm((DEC_BATCH, DEC_SEQ, D_MODEL), 1.0)
    cache_k = nrm((DEPTH, DEC_BATCH, PAST_LEN, N_KV, HEAD_DIM), 1.0)
    cache_v = nrm((DEPTH, DEC_BATCH, PAST_LEN, N_KV, HEAD_DIM), 1.0)
    cache_logf = jax.nn.log_sigmoid(FORGET_BIAS + nrm((DEPTH, DEC_BATCH, PAST_LEN, N_HEADS), 0.5))
    state_rnn_h = nrm((DEPTH, DEC_BATCH, D_RNN), 0.5)
    state_rnn_conv = nrm((DEPTH, DEC_BATCH, RNN_CONV_W - 1, D_RNN), 1.0)
    state_ffn_conv = nrm((DEPTH, DEC_BATCH, FFN_CONV_W - 1, D_FF), 1.0)
    meta_tokens = nrm((N_META, D_MODEL), 1.0)
    g_mix = 1.0 + nrm((DEPTH, D_MODEL), 0.02)
    w_in = nrm((DEPTH, D_MODEL, IN_COLS), D_MODEL ** -0.5)
    w_in = w_in.at[..., COL_F:].multiply(0.1)
    b_f = FORGET_BIAS + nrm((DEPTH, N_HEADS), 0.5)
    w_rnn_conv = nrm((DEPTH, RNN_CONV_W, D_RNN), RNN_CONV_W ** -0.5)
    b_rnn_conv = nrm((DEPTH, D_RNN), 0.01)
    w_rg_a = nrm((DEPTH, N_RNN_BLOCKS, RNN_BLOCK, RNN_BLOCK), RNN_BLOCK ** -0.5)
    b_rg_a = nrm((DEPTH, D_RNN), 0.01)
    w_rg_x = nrm((DEPTH, N_RNN_BLOCKS, RNN_BLOCK, RNN_BLOCK), RNN_BLOCK ** -0.5)
    b_rg_x = nrm((DEPTH, D_RNN), 0.01)
    a0 = jax.random.uniform(next(ks), (DEPTH, D_RNN), f32, 0.9, 0.999)
    s = a0 ** (1.0 / LRU_C)
    lru_lambda = jnp.log(s) - jnp.log1p(-s)
    g_out_rnn = 1.0 + nrm((DEPTH, D_RNN), 0.02)
    g_out_attn = 1.0 + nrm((DEPTH, D_ATTN), 0.02)
    w_out = nrm((DEPTH, D_MIX, D_MODEL), D_MIX ** -0.5)
    g_ffn = 1.0 + nrm((DEPTH, D_MODEL), 0.02)
    w_ffn_in = nrm((DEPTH, D_MODEL, 2 * D_FF), D_MODEL ** -0.5)
    w_ffn_conv = nrm((DEPTH, FFN_CONV_W, D_FF), FFN_CONV_W ** -0.5)
    b_ffn_conv = nrm((DEPTH, D_FF), 0.01)
    w_ffn_out = nrm((DEPTH, D_FF, D_MODEL), D_FF ** -0.5)
    g_final = 1.0 + nrm((D_MODEL,), 0.02)
    return {"x_prompt": x_prompt, "x_sample": x_sample, "cache_k": cache_k, "cache_v": cache_v,
            "cache_logf": cache_logf, "state_rnn_h": state_rnn_h, "state_rnn_conv": state_rnn_conv,
            "state_ffn_conv": state_ffn_conv, "meta_tokens": meta_tokens, "g_mix": g_mix, "w_in": w_in,
            "b_f": b_f, "w_rnn_conv": w_rnn_conv, "b_rnn_conv": b_rnn_conv, "w_rg_a": w_rg_a,
            "b_rg_a": b_rg_a, "w_rg_x": w_rg_x, "b_rg_x": b_rg_x, "lru_lambda": lru_lambda,
            "g_out_rnn": g_out_rnn, "g_out_attn": g_out_attn, "w_out": w_out, "g_ffn": g_ffn,
            "w_ffn_in": w_ffn_in, "w_ffn_conv": w_ffn_conv, "b_ffn_conv": b_ffn_conv,
            "w_ffn_out": w_ffn_out, "g_final": g_final}


def reference(x_prompt, x_sample, cache_k, cache_v, cache_logf, state_rnn_h, state_rnn_conv, state_ffn_conv,
              meta_tokens, g_mix, w_in, b_f, w_rnn_conv, b_rnn_conv, w_rg_a, b_rg_a, w_rg_x, b_rg_x,
              lru_lambda, g_out_rnn, g_out_attn, w_out, g_ffn, w_ffn_in, w_ffn_conv, b_ffn_conv,
              w_ffn_out, g_final):
    assert x_sample.shape[1] <= CHUNK
    B = x_prompt.shape[0]
    dt = x_prompt.dtype
    meta = jnp.broadcast_to(meta_tokens.astype(dt)[None], (B, N_META, D_MODEL))
    hp = jnp.concatenate([meta, x_prompt], axis=1)
    hs = x_sample
    new_p = []
    new_s = []
    for l in range(DEPTH):
        lp = {'g_mix': g_mix[l], 'w_in': w_in[l], 'b_f': b_f[l], 'w_rnn_conv': w_rnn_conv[l],
              'b_rnn_conv': b_rnn_conv[l], 'w_rg_a': w_rg_a[l], 'b_rg_a': b_rg_a[l], 'w_rg_x': w_rg_x[l],
              'b_rg_x': b_rg_x[l], 'lru_lambda': lru_lambda[l], 'g_out_rnn': g_out_rnn[l],
              'g_out_attn': g_out_attn[l], 'w_out': w_out[l], 'g_ffn': g_ffn[l], 'w_ffn_in': w_ffn_in[l],
              'w_ffn_conv': w_ffn_conv[l], 'b_ffn_conv': b_ffn_conv[l], 'w_ffn_out': w_ffn_out[l]}
        hp, st_p = layer(hp, lp, None, None, None,
                         jnp.zeros((B, D_RNN), jnp.float32),
                         jnp.zeros((B, RNN_CONV_W - 1, D_RNN), dt),
                         jnp.zeros((B, FFN_CONV_W - 1, D_FF), dt))
        hs, st_s = layer(hs, lp, cache_k[l], cache_v[l], cache_logf[l], state_rnn_h[l],
                         state_rnn_conv[l], state_ffn_conv[l])
        new_p.append(st_p)
        new_s.append(st_s)
    y_prompt = rms_norm(hp, g_final)[:, N_META:]
    y_sample = rms_norm(hs, g_final)
    new_k_prompt = jnp.stack([st[0] for st in new_p])
    new_v_prompt = jnp.stack([st[1] for st in new_p])
    new_logf_prompt = jnp.stack([st[2] for st in new_p])
    new_rnn_h_prompt = jnp.stack([st[3] for st in new_p])
    new_rnn_conv_prompt = jnp.stack([st[4] for st in new_p])
    new_ffn_conv_prompt = jnp.stack([st[5] for st in new_p])
    new_k_sample = jnp.stack([st[0] for st in new_s])
    new_v_sample = jnp.stack([st[1] for st in new_s])
    new_logf_sample = jnp.stack([st[2] for st in new_s])
    new_rnn_h_sample = jnp.stack([st[3] for st in new_s])
    new_rnn_conv_sample = jnp.stack([st[4] for st in new_s])
    new_ffn_conv_sample = jnp.stack([st[5] for st in new_s])
    return (y_prompt, y_sample, new_k_prompt, new_v_prompt, new_logf_prompt, new_rnn_h_prompt,
            new_rnn_conv_prompt, new_ffn_conv_prompt, new_k_sample, new_v_sample, new_logf_sample,
            new_rnn_h_sample, new_rnn_conv_sample, new_ffn_conv_sample)
```

```python
import functools

import jax
import jax.numpy as jnp
from jax import lax
from jax.experimental import pallas as pl
from jax.experimental.pallas import tpu as pltpu

F32 = jnp.float32
BF16 = jnp.bfloat16

EPS = 1e-6
LRU_C = 8.0
HEAD_DIM = 128
LANES = 128
SUBLANES = 8
NEG_BIG = -1e30
VMEM_LIMIT = 56 * 1024 * 1024


def _cparams(sem):
    return pltpu.CompilerParams(dimension_semantics=sem, vmem_limit_bytes=VMEM_LIMIT)


def _rms(x, g):
    return x * lax.rsqrt(jnp.mean(x * x, axis=-1, keepdims=True) + EPS) * g


def _softplus(x):
    return jnp.maximum(x, 0.0) + jnp.log1p(jnp.exp(-jnp.abs(x)))


def _gelu(x):
    return jax.nn.gelu(x, approximate=True)


def _split3_dot(l_bf16, x):
    hi = x.astype(BF16)
    r1 = x - hi.astype(F32)
    mid = r1.astype(BF16)
    lo = (r1 - mid.astype(F32)).astype(BF16)
    out = jnp.dot(l_bf16, hi, preferred_element_type=F32)
    out = out + jnp.dot(l_bf16, mid, preferred_element_type=F32)
    return out + jnp.dot(l_bf16, lo, preferred_element_type=F32)


def _lower_tri(n):
    r = lax.broadcasted_iota(jnp.int32, (n, n), 0)
    c = lax.broadcasted_iota(jnp.int32, (n, n), 1)
    return jnp.where(c <= r, 1.0, 0.0).astype(BF16)


def _inproj_kernel(x_ref, g_ref, w_ref, wf_ref, bf_ref, u_ref, lf_ref, xn_ref):
    @pl.when(pl.program_id(1) == 0)
    def _():
        xb = _rms(x_ref[...], g_ref[...]).astype(BF16)
        xn_ref[...] = xb
        f = jnp.dot(xb, wf_ref[...], preferred_element_type=F32) + bf_ref[...]
        lf_ref[...] = -_softplus(-f)

    u_ref[...] = jnp.dot(xn_ref[...], w_ref[...], preferred_element_type=F32)


def _inproj(x, g, w, wf, bf, *, tm, tn):
    m, d = x.shape
    nc = w.shape[1]
    return pl.pallas_call(
        _inproj_kernel,
        grid=(m // tm, nc // tn),
        in_specs=[
            pl.BlockSpec((tm, d), lambda i, j: (i, 0)),
            pl.BlockSpec((1, d), lambda i, j: (0, 0)),
            pl.BlockSpec((d, tn), lambda i, j: (0, j)),
            pl.BlockSpec((d, LANES), lambda i, j: (0, 0)),
            pl.BlockSpec((1, LANES), lambda i, j: (0, 0)),
        ],
        out_specs=[
            pl.BlockSpec((tm, tn), lambda i, j: (i, j)),
            pl.BlockSpec((tm, LANES), lambda i, j: (i, 0)),
        ],
        out_shape=[jax.ShapeDtypeStruct((m, nc), F32), jax.ShapeDtypeStruct((m, LANES), F32)],
        scratch_shapes=[pltpu.VMEM((tm, d), BF16)],
        compiler_params=_cparams(("arbitrary", "arbitrary")),
        name="inproj",
    )(x, g, w, wf, bf)


def _prev_rows(xp_s, x, first_rows, states, *, rows, seq_len, multi_seq):
    i = pl.program_id(0)
    if multi_seq:
        xp_s[0:SUBLANES, :] = jnp.zeros((SUBLANES, x.shape[1]), F32)
    else:
        @pl.when(i == 0)
        def _():
            xp_s[0:SUBLANES, :] = first_rows[...]
    xp_s[SUBLANES:, :] = x
    prevs = []
    for k in range(1, len(states) + 1):
        pk = xp_s[pl.ds(SUBLANES - k, rows), :]
        if multi_seq:
            pos = lax.broadcasted_iota(jnp.int32, (rows, 1), 0) % seq_len
            pk = jnp.where(pos >= k, pk, states[k - 1][...])
        prevs.append(pk)
    return prevs


def _rglru_kernel(xr_ref, yg_ref, first_ref, p1_ref, p2_ref, p3_ref, h0_ref, wc_ref, bc_ref, wg_ref, ba_ref,
                  bx_ref, lam_ref, go_ref, o_ref, hl_ref, xp_s, a_s, b_s, h_s, *, tr, seq_len, n_blocks,
                  multi_seq):
    i = pl.program_id(0)
    d = xr_ref.shape[1]
    blk = d // n_blocks
    x = xr_ref[...]

    @pl.when(i == 0)
    def _():
        h_s[...] = jnp.zeros_like(h_s)

    prev1, prev2, prev3 = _prev_rows(xp_s, x, first_ref, (p1_ref, p2_ref, p3_ref), rows=tr, seq_len=seq_len,
                                     multi_seq=multi_seq)
    xc = bc_ref[...] + wc_ref[0:1, :] * prev3
    xc = xc + wc_ref[1:2, :] * prev2
    xc = xc + wc_ref[2:3, :] * prev1
    xc = xc + wc_ref[3:4, :] * x
    xp_s[0:SUBLANES, :] = x[tr - SUBLANES:, :]

    sp = _softplus(-lam_ref[...])
    xcb = xc.astype(BF16)
    for n in range(n_blocks):
        sl = slice(n * blk, (n + 1) * blk)
        gn = jnp.dot(xcb[:, sl], wg_ref[n], preferred_element_type=F32)
        r = jax.nn.sigmoid(gn[:, :blk] + ba_ref[:, sl])
        ig = jax.nn.sigmoid(gn[:, blk:] + bx_ref[:, sl])
        log_a = (-LRU_C) * r * sp[:, sl]
        a = jnp.exp(log_a)
        a_s[:, sl] = a
        b_s[:, sl] = jnp.sqrt(-jnp.tanh(log_a) * (a * a + 1.0)) * (ig * xc[:, sl])

    row = lax.broadcasted_iota(jnp.int32, (SUBLANES, d), 0)
    groups_per_seq = seq_len // SUBLANES

    def group(gi, h):
        r0 = pl.multiple_of(gi * SUBLANES, SUBLANES)
        gg = i * (tr // SUBLANES) + gi
        seq = gg // groups_per_seq
        first = (gg % groups_per_seq) == 0
        h = jnp.where(first, h0_ref[pl.ds(seq, 1), :], h)
        a8 = a_s[pl.ds(r0, SUBLANES), :]
        b8 = b_s[pl.ds(r0, SUBLANES), :]
        for sh in (1, 2, 4):
            ok = row >= sh
            b8 = jnp.where(ok, a8 * pltpu.roll(b8, sh, axis=0) + b8, b8)
            a8 = jnp.where(ok, a8 * pltpu.roll(a8, sh, axis=0), a8)
        h8 = a8 * h + b8
        b_s[pl.ds(r0, SUBLANES), :] = h8
        hn = h8[SUBLANES - 1:SUBLANES, :]

        @pl.when((gg % groups_per_seq) == groups_per_seq - 1)
        def _():
            hl_ref[pl.ds(seq, 1), :] = hn
        return hn

    h_s[...] = lax.fori_loop(0, tr // SUBLANES, group, h_s[...])

    o = b_s[...] * _gelu(yg_ref[...])
    o_ref[...] = _rms(o, go_ref[...]).astype(BF16)


def _conv_state_inputs(state, *, rows, seq_len, multi_seq):
    n_seq, w, c = state.shape
    dummy = jnp.zeros((SUBLANES, c), F32)
    if multi_seq:
        assert rows == n_seq * seq_len
        first = dummy
        states = [_place_state(state, seq_len, k) for k in range(1, w + 1)]
    else:
        assert n_seq == 1
        first = jnp.pad(state[0], ((SUBLANES - w, 0), (0, 0)))
        states = [dummy] * w
    return first, states


def _rglru(u, rnn_state, h0, wc, bc, wg, ba, bx, lam, go, *, tr, seq_len, d_rnn):
    m = u.shape[0]
    n_seq = m // seq_len
    multi_seq = n_seq > 1
    n_blocks = wg.shape[0]
    first, states = _conv_state_inputs(rnn_state, rows=tr, seq_len=seq_len, multi_seq=multi_seq)
    row_spec = lambda c: pl.BlockSpec((tr, d_rnn), lambda i: (i, c))
    full = lambda a: pl.BlockSpec(a.shape, lambda i: (0,) * a.ndim)
    kern = functools.partial(_rglru_kernel, tr=tr, seq_len=seq_len, n_blocks=n_blocks, multi_seq=multi_seq)
    return pl.pallas_call(
        kern,
        grid=(m // tr,),
        in_specs=[row_spec(0), row_spec(1), full(first)] + [full(s) for s in states] + [
            full(h0), full(wc), full(bc), full(wg), full(ba), full(bx), full(lam), full(go)],
        out_specs=[pl.BlockSpec((tr, d_rnn), lambda i: (i, 0)), pl.BlockSpec((n_seq, d_rnn), lambda i: (0, 0))],
        out_shape=[jax.ShapeDtypeStruct((m, d_rnn), BF16), jax.ShapeDtypeStruct((n_seq, d_rnn), F32)],
        scratch_shapes=[pltpu.VMEM((tr + SUBLANES, d_rnn), F32), pltpu.VMEM((tr, d_rnn), F32),
                        pltpu.VMEM((tr, d_rnn), F32), pltpu.VMEM((1, d_rnn), F32)],
        compiler_params=_cparams(("arbitrary",)),
        name="rglru",
    )(u, u, first, *states, h0, wc, bc, wg, ba, bx, lam, go)


def _prep_kernel(q_ref, k_ref, v_ref, lf_ref, c0_ref, qb_ref, kb_ref, vb_ref, cq_ref, ckt_ref, carry_s,
                 *, tm, n_kv, group):
    @pl.when(pl.program_id(0) == 0)
    def _():
        carry_s[...] = c0_ref[...]

    qb_ref[...] = (q_ref[...] * (HEAD_DIM ** -0.5)).astype(BF16)
    kb_ref[...] = k_ref[...].astype(BF16)
    vb_ref[...] = v_ref[...].astype(BF16)
    c = _split3_dot(_lower_tri(tm), lf_ref[...]) + carry_s[...]
    carry_s[...] = c[tm - 1:tm, :]
    ct = c.T
    for g in range(n_kv):
        h0 = g * group
        a0 = h0 // SUBLANES * SUBLANES
        rows8 = ct[a0:a0 + SUBLANES, :]
        ckt_ref[g] = rows8 if h0 == a0 else pltpu.roll(rows8, SUBLANES - (h0 - a0), axis=0)
        cq_ref[g] = c if g == 0 else pltpu.roll(c, LANES - h0, axis=1)


def _prep(u, lf, c0, *, tm, col_q, d_attn, n_kv, group):
    m = u.shape[0]
    dkv = n_kv * HEAD_DIM
    cq_blk, ck_blk, cv_blk = col_q // d_attn, (col_q + d_attn) // dkv, (col_q + d_attn + dkv) // dkv
    kern = functools.partial(_prep_kernel, tm=tm, n_kv=n_kv, group=group)
    return pl.pallas_call(
        kern,
        grid=(m // tm,),
        in_specs=[
            pl.BlockSpec((tm, d_attn), lambda i: (i, cq_blk)),
            pl.BlockSpec((tm, dkv), lambda i: (i, ck_blk)),
            pl.BlockSpec((tm, dkv), lambda i: (i, cv_blk)),
            pl.BlockSpec((tm, LANES), lambda i: (i, 0)),
            pl.BlockSpec((1, LANES), lambda i: (0, 0)),
        ],
        out_specs=[
            pl.BlockSpec((tm, d_attn), lambda i: (i, 0)),
            pl.BlockSpec((tm, dkv), lambda i: (i, 0)),
            pl.BlockSpec((tm, dkv), lambda i: (i, 0)),
            pl.BlockSpec((n_kv, tm, LANES), lambda i: (0, i, 0)),
            pl.BlockSpec((n_kv, SUBLANES, tm), lambda i: (0, 0, i)),
        ],
        out_shape=[
            jax.ShapeDtypeStruct((m, d_attn), BF16),
            jax.ShapeDtypeStruct((m, dkv), BF16),
            jax.ShapeDtypeStruct((m, dkv), BF16),
            jax.ShapeDtypeStruct((n_kv, m, LANES), F32),
            jax.ShapeDtypeStruct((n_kv, SUBLANES, m), F32),
        ],
        scratch_shapes=[pltpu.VMEM((1, LANES), F32)],
        compiler_params=_cparams(("arbitrary",)),
        name="attn_prep",
    )(u, u, u, lf, c0)


def _online_update(r, s, v, m_s, l_s, acc_s):
    m_prev = m_s[r]
    m_new = jnp.maximum(m_prev, jnp.max(s, axis=1, keepdims=True))
    alpha = jnp.exp(m_prev - m_new)
    p = jnp.exp(s - m_new)
    l_s[r] = alpha * l_s[r] + jnp.sum(p, axis=1, keepdims=True)
    acc_s[r] = alpha * acc_s[r] + jnp.dot(p.astype(BF16), v, preferred_element_type=F32)
    m_s[r] = m_new


def _qk(q, k):
    return lax.dot_general(q, k, (((1,), (1,)), ((), ())), preferred_element_type=F32)


def _attn_kernel(q_ref, k_ref, v_ref, cq_ref, ck_ref, km_ref, vm_ref, ckm_ref, o_ref, m_s, l_s, acc_s,
                 *, tq, tk, group, n_prefix):
    qi = pl.program_id(1)
    ki = pl.program_id(2)
    cq = cq_ref[...]

    @pl.when(ki == 0)
    def _():
        lane = lax.broadcasted_iota(jnp.int32, (tq, km_ref.shape[0]), 1)
        for r in range(group):
            m_s[r] = jnp.full((tq, 1), NEG_BIG, F32)
            l_s[r] = jnp.zeros((tq, 1), F32)
            acc_s[r] = jnp.zeros((tq, HEAD_DIM), F32)
            s = _qk(q_ref[:, r * HEAD_DIM:(r + 1) * HEAD_DIM], km_ref[...])
            s = s + (cq[:, r:r + 1] - ckm_ref[r:r + 1, :])
            s = jnp.where(lane < n_prefix, s, NEG_BIG)
            _online_update(r, s, vm_ref[...], m_s, l_s, acc_s)

    def step(masked):
        if masked:
            rows = qi * tq + lax.broadcasted_iota(jnp.int32, (tq, tk), 0)
            cols = ki * tk + lax.broadcasted_iota(jnp.int32, (tq, tk), 1)
            keep = cols <= rows
        for r in range(group):
            s = _qk(q_ref[:, r * HEAD_DIM:(r + 1) * HEAD_DIM], k_ref[...])
            s = s + (cq[:, r:r + 1] - ck_ref[r:r + 1, :])
            if masked:
                s = jnp.where(keep, s, NEG_BIG)
            _online_update(r, s, v_ref[...], m_s, l_s, acc_s)

    last = ((qi + 1) * tq - 1) // tk

    @pl.when(ki * tk + tk - 1 <= qi * tq)
    def _():
        step(False)

    @pl.when(jnp.logical_and(ki * tk + tk - 1 > qi * tq, ki <= last))
    def _():
        step(True)

    @pl.when(ki == last)
    def _():
        for r in range(group):
            o_ref[:, r * HEAD_DIM:(r + 1) * HEAD_DIM] = acc_s[r] / l_s[r]


def _attn(qb, kb, vb, cq, ckt, km, vm, ckm, *, tq, tk, n_kv, group, n_prefix):
    m, d_attn = qb.shape
    gw = group * HEAD_DIM
    npad = km.shape[0]

    def kv_map(g, qi, ki):
        return (jnp.minimum(ki, ((qi + 1) * tq - 1) // tk), g)

    def ck_map(g, qi, ki):
        return (g, 0, jnp.minimum(ki, ((qi + 1) * tq - 1) // tk))

    kern = functools.partial(_attn_kernel, tq=tq, tk=tk, group=group, n_prefix=n_prefix)
    return pl.pallas_call(
        kern,
        grid=(n_kv, m // tq, m // tk),
        in_specs=[
            pl.BlockSpec((tq, gw), lambda g, qi, ki: (qi, g)),
            pl.BlockSpec((tk, HEAD_DIM), kv_map),
            pl.BlockSpec((tk, HEAD_DIM), kv_map),
            pl.BlockSpec((None, tq, LANES), lambda g, qi, ki: (g, qi, 0)),
            pl.BlockSpec((None, SUBLANES, tk), ck_map),
            pl.BlockSpec((npad, HEAD_DIM), lambda g, qi, ki: (0, g)),
            pl.BlockSpec((npad, HEAD_DIM), lambda g, qi, ki: (0, g)),
            pl.BlockSpec((None, SUBLANES, npad), lambda g, qi, ki: (g, 0, 0)),
        ],
        out_specs=pl.BlockSpec((tq, gw), lambda g, qi, ki: (qi, g)),
        out_shape=jax.ShapeDtypeStruct((m, d_attn), F32),
        scratch_shapes=[pltpu.VMEM((group, tq, 1), F32), pltpu.VMEM((group, tq, 1), F32),
                        pltpu.VMEM((group, tq, HEAD_DIM), F32)],
        compiler_params=_cparams(("arbitrary", "arbitrary", "arbitrary")),
        name="fox_attn",
    )(qb, kb, vb, cq, ckt, km, vm, ckm)


def _attn_small_kernel(q_ref, k_ref, v_ref, lf_ref, ck_ref, cv_ref, clf_ref, o_ref, c_ref, ccum_s,
                       *, n_kv, group, n_cached_seqs, p_len, s_len):
    b = pl.program_id(0)
    has_cache = b < n_cached_seqs
    n_blk = p_len // LANES
    tri = _lower_tri(LANES)

    def cblock(j, off):
        r0 = pl.multiple_of(j * LANES, LANES)
        cb = _split3_dot(tri, clf_ref[pl.ds(r0, LANES), :]) + off
        ccum_s[pl.ds(r0, LANES), :] = cb
        return cb[LANES - 1:LANES, :]

    total = lax.fori_loop(0, n_blk, cblock, jnp.zeros((1, LANES), F32))
    total = jnp.where(has_cache, total, 0.0)
    zrows = jnp.zeros((LANES - s_len, LANES), F32)
    c_pad = _split3_dot(tri, jnp.concatenate([lf_ref[...], zrows], axis=0)) + total
    c_own = c_pad[:s_len, :]
    c_ref[...] = c_own
    c_own_t = c_pad.T
    c_cache_t = ccum_s[...].T

    rows = lax.broadcasted_iota(jnp.int32, (s_len, LANES), 0)
    cols = lax.broadcasted_iota(jnp.int32, (s_len, LANES), 1)
    scale = HEAD_DIM ** -0.5
    for g in range(n_kv):
        ksl = slice(g * HEAD_DIM, (g + 1) * HEAD_DIM)
        k_own = jnp.concatenate([k_ref[:, ksl], zrows], axis=0).astype(BF16)
        v_own = jnp.concatenate([v_ref[:, ksl], zrows], axis=0).astype(BF16)
        k_cache = ck_ref[:, ksl].astype(BF16)
        v_cache = cv_ref[:, ksl].astype(BF16)
        for r in range(group):
            h = g * group + r
            q = (q_ref[:, h * HEAD_DIM:(h + 1) * HEAD_DIM] * scale).astype(BF16)
            cqh = c_own[:, h:h + 1]
            s_own = _qk(q, k_own) + (cqh - c_own_t[h:h + 1, :])
            s_own = jnp.where(cols <= rows, s_own, NEG_BIG)
            s_cache = _qk(q, k_cache) + (cqh - c_cache_t[h:h + 1, :])
            s_cache = jnp.where(has_cache, s_cache, NEG_BIG)
            mx = jnp.maximum(jnp.max(s_own, axis=1, keepdims=True), jnp.max(s_cache, axis=1, keepdims=True))
            p_own = jnp.exp(s_own - mx)
            p_cache = jnp.exp(s_cache - mx)
            den = jnp.sum(p_own, axis=1, keepdims=True) + jnp.sum(p_cache, axis=1, keepdims=True)
            num = jnp.dot(p_own.astype(BF16), v_own, preferred_element_type=F32)
            num = num + jnp.dot(p_cache.astype(BF16), v_cache, preferred_element_type=F32)
            o_ref[:, h * HEAD_DIM:(h + 1) * HEAD_DIM] = num / den


def _attn_small(u, lf, cache_k, cache_v, cache_lf, *, n_seq, s_len, col_q, d_attn, n_kv, group):
    n_cached, p_len, dkv = cache_k.shape
    ck_blk, cv_blk = (col_q + d_attn) // dkv, (col_q + d_attn + dkv) // dkv
    cq_blk = col_q // d_attn
    cmap = lambda b: (jnp.minimum(b, n_cached - 1), 0, 0)
    kern = functools.partial(_attn_small_kernel, n_kv=n_kv, group=group, n_cached_seqs=n_cached, p_len=p_len,
                             s_len=s_len)
    return pl.pallas_call(
        kern,
        grid=(n_seq,),
        in_specs=[
            pl.BlockSpec((s_len, d_attn), lambda b: (b, cq_blk)),
            pl.BlockSpec((s_len, dkv), lambda b: (b, ck_blk)),
            pl.BlockSpec((s_len, dkv), lambda b: (b, cv_blk)),
            pl.BlockSpec((s_len, LANES), lambda b: (b, 0)),
            pl.BlockSpec((None, p_len, dkv), cmap),
            pl.BlockSpec((None, p_len, dkv), cmap),
            pl.BlockSpec((None, p_len, LANES), cmap),
        ],
        out_specs=[pl.BlockSpec((s_len, d_attn), lambda b: (b, 0)),
                   pl.BlockSpec((s_len, LANES), lambda b: (b, 0))],
        out_shape=[jax.ShapeDtypeStruct((n_seq * s_len, d_attn), F32),
                   jax.ShapeDtypeStruct((n_seq * s_len, LANES), F32)],
        scratch_shapes=[pltpu.VMEM((p_len, LANES), F32)],
        compiler_params=_cparams(("arbitrary",)),
        name="fox_attn_small",
    )(u, u, u, lf, cache_k, cache_v, cache_lf)


def _outproj_kernel(orn_ref, oat_ref, ga_ref, w_ref, x_ref, h_ref, a_s, *, d_rnn):
    @pl.when(pl.program_id(1) == 0)
    def _():
        a_s[:, :d_rnn] = orn_ref[...]
        a_s[:, d_rnn:] = _rms(oat_ref[...], ga_ref[...]).astype(BF16)

    h_ref[...] = x_ref[...] + jnp.dot(a_s[...], w_ref[...], preferred_element_type=F32)


def _outproj(orn, oat, ga, w, x, *, tm, tn):
    m, d_rnn = orn.shape
    d_attn = oat.shape[1]
    d = x.shape[1]
    return pl.pallas_call(
        functools.partial(_outproj_kernel, d_rnn=d_rnn),
        grid=(m // tm, d // tn),
        in_specs=[
            pl.BlockSpec((tm, d_rnn), lambda i, j: (i, 0)),
            pl.BlockSpec((tm, d_attn), lambda i, j: (i, 0)),
            pl.BlockSpec((1, d_attn), lambda i, j: (0, 0)),
            pl.BlockSpec((d_rnn + d_attn, tn), lambda i, j: (0, j)),
            pl.BlockSpec((tm, tn), lambda i, j: (i, j)),
        ],
        out_specs=pl.BlockSpec((tm, tn), lambda i, j: (i, j)),
        out_shape=jax.ShapeDtypeStruct((m, d), F32),
        scratch_shapes=[pltpu.VMEM((tm, d_rnn + d_attn), BF16)],
        compiler_params=_cparams(("arbitrary", "arbitrary")),
        name="outproj",
    )(orn, oat, ga, w, x)


def _ffn_in_kernel(h_ref, g_ref, wg_ref, wv_ref, wc_ref, bc_ref, first_ref, p1_ref, p2_ref, act_ref, zt_ref,
                   xn_s, zp_s, carry_s, *, tm, seq_len, multi_seq):
    i = pl.program_id(0)
    f = pl.program_id(1)

    @pl.when(f == 0)
    def _():
        xn_s[...] = _rms(h_ref[...], g_ref[...]).astype(BF16)

    xn = xn_s[...]
    zg = jnp.dot(xn, wg_ref[...], preferred_element_type=F32)
    zv = jnp.dot(xn, wv_ref[...], preferred_element_type=F32)

    if not multi_seq:
        @pl.when(i > 0)
        def _():
            zp_s[0:SUBLANES, :] = carry_s[f]
    prev1, prev2 = _prev_rows(zp_s, zg, first_ref, (p1_ref, p2_ref), rows=tm, seq_len=seq_len,
                              multi_seq=multi_seq)
    gate = bc_ref[...] + wc_ref[0:1, :] * prev2
    gate = gate + wc_ref[1:2, :] * prev1
    gate = gate + wc_ref[2:3, :] * zg
    if multi_seq:
        zt_ref[...] = zg
    else:
        tail = zg[tm - SUBLANES:, :]
        carry_s[f] = tail
        zt_ref[...] = tail
    act_ref[...] = (_gelu(gate) * zv).astype(BF16)


def _ffn_in(h, g, w, wc, bc, ffn_state, *, tm, tf, seq_len):
    m, d = h.shape
    d_ff = w.shape[1] // 2
    nf = d_ff // tf
    multi_seq = m // seq_len > 1
    first, states = _conv_state_inputs(ffn_state, rows=tm, seq_len=seq_len, multi_seq=multi_seq)
    zt_rows = tm if multi_seq else SUBLANES
    st_spec = lambda a: pl.BlockSpec((a.shape[0], tf), lambda i, f: (0, f))
    return pl.pallas_call(
        functools.partial(_ffn_in_kernel, tm=tm, seq_len=seq_len, multi_seq=multi_seq),
        grid=(m // tm, nf),
        in_specs=[
            pl.BlockSpec((tm, d), lambda i, f: (i, 0)),
            pl.BlockSpec((1, d), lambda i, f: (0, 0)),
            pl.BlockSpec((d, tf), lambda i, f: (0, f)),
            pl.BlockSpec((d, tf), lambda i, f: (0, f + nf)),
            pl.BlockSpec((wc.shape[0], tf), lambda i, f: (0, f)),
            pl.BlockSpec((1, tf), lambda i, f: (0, f)),
            st_spec(first)] + [st_spec(s) for s in states],
        out_specs=[pl.BlockSpec((tm, tf), lambda i, f: (i, f)),
                   pl.BlockSpec((zt_rows, tf), lambda i, f: (i, f))],
        out_shape=[jax.ShapeDtypeStruct((m, d_ff), BF16),
                   jax.ShapeDtypeStruct((m // tm * zt_rows, d_ff), F32)],
        scratch_shapes=[pltpu.VMEM((tm, d), BF16), pltpu.VMEM((tm + SUBLANES, tf), F32),
                        pltpu.VMEM((nf, SUBLANES, tf), F32)],
        compiler_params=_cparams(("arbitrary", "arbitrary")),
        name="ffn_in",
    )(h, g, w, w, wc, bc, first, *states)


def _ffn_out_kernel(act_ref, w_ref, h_ref, g_ref, y_ref, *, tc):
    k = pl.program_id(1)
    d = y_ref.shape[1]
    chunks = [slice(c, c + tc) for c in range(0, d, tc)]

    @pl.when(k == 0)
    def _():
        y_ref[...] = h_ref[...]

    act = act_ref[...]
    for sl in chunks:
        y_ref[:, sl] += jnp.dot(act, w_ref[:, sl], preferred_element_type=F32)

    @pl.when(k == pl.num_programs(1) - 1)
    def _():
        ssq = jnp.zeros((y_ref.shape[0], 1), F32)
        for sl in chunks:
            yc = y_ref[:, sl]
            ssq = ssq + jnp.sum(yc * yc, axis=-1, keepdims=True)
        inv = lax.rsqrt(ssq / d + EPS)
        for sl in chunks:
            y_ref[:, sl] = y_ref[:, sl] * inv * g_ref[:, sl]


def _ffn_out(act, w, h, g, *, tm, tk):
    m, d_ff = act.shape
    d = h.shape[1]
    return pl.pallas_call(
        functools.partial(_ffn_out_kernel, tc=_pick(d, 1024)),
        grid=(m // tm, d_ff // tk),
        in_specs=[
            pl.BlockSpec((tm, tk), lambda i, k: (i, k)),
            pl.BlockSpec((tk, d), lambda i, k: (k, 0)),
            pl.BlockSpec((tm, d), lambda i, k: (i, 0)),
            pl.BlockSpec((1, d), lambda i, k: (0, 0)),
        ],
        out_specs=pl.BlockSpec((tm, d), lambda i, k: (i, 0)),
        out_shape=jax.ShapeDtypeStruct((m, d), F32),
        compiler_params=_cparams(("arbitrary", "arbitrary")),
        name="ffn_out",
    )(act, w, h, g)


def _pick(n, pref):
    t = min(n, pref)
    while n % t:
        t //= 2
    return t


def _place_state(state, seq_len, back):
    n_seq, w, c = state.shape
    out = jnp.zeros((n_seq, seq_len, c), state.dtype)
    out = out.at[:, :back, :].set(state[:, w - back:, :])
    return out.reshape(n_seq * seq_len, c)


def kernel(x_prompt, x_sample, cache_k, cache_v, cache_logf, state_rnn_h, state_rnn_conv, state_ffn_conv,
           meta_tokens, g_mix, w_in, b_f, w_rnn_conv, b_rnn_conv, w_rg_a, b_rg_a, w_rg_x, b_rg_x, lru_lambda,
           g_out_rnn, g_out_attn, w_out, g_ffn, w_ffn_in, w_ffn_conv, b_ffn_conv, w_ffn_out, g_final):
    depth = g_mix.shape[0]
    assert depth == 1, "single-layer stack"
    batch, seq, d_model = x_prompt.shape
    assert batch == 1
    dec_batch, dec_seq, _ = x_sample.shape
    n_meta = meta_tokens.shape[0]
    assert n_meta == dec_seq, "meta prefix is run as one more sample-length sequence"
    n_heads = b_f.shape[1]
    n_kv = cache_k.shape[3]
    group = n_heads // n_kv
    assert group <= SUBLANES and cache_k.shape[4] == HEAD_DIM
    d_attn = n_heads * HEAD_DIM
    d_rnn = state_rnn_h.shape[2]
    dkv = n_kv * HEAD_DIM
    d_ff = state_ffn_conv.shape[3]
    past = cache_k.shape[2]
    col_q = 2 * d_rnn
    n_main_cols = col_q + d_attn + 2 * dkv
    n_rnn_blocks = w_rg_a.shape[1]
    rnn_w = w_rnn_conv.shape[1]
    ffn_w = w_ffn_conv.shape[1]
    assert rnn_w == 4 and ffn_w == 3

    w_in_b = w_in[0][:, :n_main_cols].astype(BF16)
    wf_b = jnp.pad(w_in[0][:, n_main_cols:], ((0, 0), (0, LANES - n_heads))).astype(BF16)
    bf_p = jnp.pad(b_f[0], (0, LANES - n_heads)).reshape(1, LANES)
    w_out_b = w_out[0].astype(BF16)
    w_ffn_in_b = w_ffn_in[0].astype(BF16)
    w_ffn_out_b = w_ffn_out[0].astype(BF16)
    wg = jnp.concatenate([w_rg_a[0], w_rg_x[0]], axis=-1).astype(BF16)
    row = lambda a: a.reshape(1, -1)
    rg_args = (w_rnn_conv[0], row(b_rnn_conv[0]), wg, row(b_rg_a[0]), row(b_rg_x[0]), row(lru_lambda[0]),
               row(g_out_rnn[0]))

    def chain(x, seq_len, rnn_state, h0, ffn_state, attn_fn, tm, tr, tn, tf, tk):
        u, lf = _inproj(x, row(g_mix[0]), w_in_b, wf_b, bf_p, tm=tm, tn=tn)
        orn, h_last = _rglru(u, rnn_state, h0, *rg_args, tr=tr, seq_len=seq_len, d_rnn=d_rnn)
        oat, attn_aux = attn_fn(u, lf)
        h1 = _outproj(orn, oat, row(g_out_attn[0]), w_out_b, x, tm=tm, tn=_pick(d_model, tn))
        act, zt = _ffn_in(h1, row(g_ffn[0]), w_ffn_in_b, w_ffn_conv[0], row(b_ffn_conv[0]), ffn_state,
                          tm=tm, tf=tf, seq_len=seq_len)
        y = _ffn_out(act, w_ffn_out_b, h1, row(g_final), tm=tm, tk=tk)
        return u, lf, h_last, zt, y, attn_aux

    n_small = dec_batch + 1
    ms = n_small * dec_seq
    xs = jnp.concatenate([x_sample.reshape(dec_batch * dec_seq, d_model), meta_tokens.astype(F32)], axis=0)
    zpad = lambda a: jnp.concatenate([a, jnp.zeros((1,) + a.shape[1:], a.dtype)], axis=0)
    clf = jnp.pad(cache_logf[0], ((0, 0), (0, 0), (0, LANES - n_heads)))

    def small_attn(u, lf):
        return _attn_small(u, lf, cache_k[0].reshape(dec_batch, past, dkv), cache_v[0].reshape(dec_batch, past, dkv),
                           clf, n_seq=n_small, s_len=dec_seq, col_q=col_q, d_attn=d_attn, n_kv=n_kv, group=group)

    tn, tf, tk = _pick(n_main_cols, 1024), _pick(d_ff, 512), _pick(d_ff, 512)
    u_s, lf_s, hl_s, zt_s, y_s, c_s = chain(
        xs, dec_seq, zpad(state_rnn_conv[0]), zpad(state_rnn_h[0]), zpad(state_ffn_conv[0]), small_attn,
        tm=ms, tr=ms, tn=tn, tf=tf, tk=tk)

    m0 = dec_batch * dec_seq
    u_meta = u_s[m0:]
    k_meta = u_meta[:, col_q + d_attn:col_q + d_attn + dkv]
    v_meta = u_meta[:, col_q + d_attn + dkv:]
    c_meta = c_s[m0:]
    npad = LANES
    km = jnp.pad(k_meta, ((0, npad - n_meta), (0, 0))).astype(BF16)
    vm = jnp.pad(v_meta, ((0, npad - n_meta), (0, 0))).astype(BF16)
    ckm = jnp.pad(c_meta[:, :n_heads].T.reshape(n_kv, group, n_meta),
                  ((0, 0), (0, SUBLANES - group), (0, npad - n_meta)))
    tq = _pick(seq, 512)

    def main_attn(u, lf):
        qb, kb, vb, cq, ckt = _prep(u, lf, c_meta[n_meta - 1:], tm=tq, col_q=col_q, d_attn=d_attn, n_kv=n_kv,
                                    group=group)
        o = _attn(qb, kb, vb, cq, ckt, km, vm, ckm, tq=tq, tk=tq, n_kv=n_kv, group=group, n_prefix=n_meta)
        return o, None

    u_m, lf_m, hl_m, zt_m, y_m, _ = chain(
        x_prompt[0], seq, u_meta[None, n_meta - (rnn_w - 1):, :d_rnn], hl_s[dec_batch:],
        zt_s[None, ms - (ffn_w - 1):, :], main_attn,
        tm=_pick(seq, 512), tr=_pick(seq, 256), tn=tn, tf=tf, tk=tk)

    kcols = slice(col_q + d_attn, col_q + d_attn + dkv)
    vcols = slice(col_q + d_attn + dkv, n_main_cols)
    y_prompt = y_m[None]
    y_sample = y_s[:m0].reshape(dec_batch, dec_seq, d_model)
    new_k_prompt = jnp.concatenate([k_meta, u_m[:, kcols]], axis=0).reshape(1, 1, n_meta + seq, n_kv, HEAD_DIM)
    new_v_prompt = jnp.concatenate([v_meta, u_m[:, vcols]], axis=0).reshape(1, 1, n_meta + seq, n_kv, HEAD_DIM)
    new_logf_prompt = jnp.concatenate([lf_s[m0:, :n_heads], lf_m[:, :n_heads]], axis=0)[None, None]
    new_rnn_h_prompt = hl_m[None]
    new_rnn_conv_prompt = u_m[seq - (rnn_w - 1):, :d_rnn][None, None]
    new_ffn_conv_prompt = zt_m[zt_m.shape[0] - (ffn_w - 1):][None, None]
    us3 = u_s[:m0].reshape(dec_batch, dec_seq, n_main_cols)
    new_k_sample = us3[:, :, kcols].reshape(1, dec_batch, dec_seq, n_kv, HEAD_DIM)
    new_v_sample = us3[:, :, vcols].reshape(1, dec_batch, dec_seq, n_kv, HEAD_DIM)
    new_logf_sample = lf_s[:m0, :n_heads].reshape(1, dec_batch, dec_seq, n_heads)
    new_rnn_h_sample = hl_s[:dec_batch][None]
    new_rnn_conv_sample = us3[:, dec_seq - (rnn_w - 1):, :d_rnn][None]
    new_ffn_conv_sample = zt_s[:m0].reshape(dec_batch, dec_seq, d_ff)[:, dec_seq - (ffn_w - 1):][None]
    return (y_prompt, y_sample, new_k_prompt, new_v_prompt, new_logf_prompt, new_rnn_h_prompt,
            new_rnn_conv_prompt, new_ffn_conv_prompt, new_k_sample, new_v_sample, new_logf_sample,
            new_rnn_h_sample, new_rnn_conv_sample, new_ffn_conv_sample)
```

```python
import functools

import jax
import jax.numpy as jnp
from jax import lax
from jax.experimental import pallas as pl
from jax.experimental.pallas import tpu as pltpu

F32 = jnp.float32
BF16 = jnp.bfloat16

EPS = 1e-6
LRU_C = 8.0
HEAD_DIM = 128
LANES = 128
SUBLANES = 8
NEG_BIG = -1e30
VMEM_LIMIT = 56 * 1024 * 1024


def _cparams(sem):
    return pltpu.CompilerParams(dimension_semantics=sem, vmem_limit_bytes=VMEM_LIMIT)


def _rms(x, g):
    return x * lax.rsqrt(jnp.mean(x * x, axis=-1, keepdims=True) + EPS) * g


def _softplus(x):
    return jnp.maximum(x, 0.0) + jnp.log1p(jnp.exp(-jnp.abs(x)))


def _gelu(x):
    return jax.nn.gelu(x, approximate=True)


def _split3_dot(l_bf16, x):
    hi = x.astype(BF16)
    r1 = x - hi.astype(F32)
    mid = r1.astype(BF16)
    lo = (r1 - mid.astype(F32)).astype(BF16)
    out = jnp.dot(l_bf16, hi, preferred_element_type=F32)
    out = out + jnp.dot(l_bf16, mid, preferred_element_type=F32)
    return out + jnp.dot(l_bf16, lo, preferred_element_type=F32)


def _lower_tri(n):
    r = lax.broadcasted_iota(jnp.int32, (n, n), 0)
    c = lax.broadcasted_iota(jnp.int32, (n, n), 1)
    return jnp.where(c <= r, 1.0, 0.0).astype(BF16)


def _inproj_kernel(x_ref, g_ref, w_ref, wf_ref, bf_ref, u_ref, lf_ref, xn_ref):
    @pl.when(pl.program_id(1) == 0)
    def _():
        xb = _rms(x_ref[...], g_ref[...]).astype(BF16)
        xn_ref[...] = xb
        f = jnp.dot(xb, wf_ref[...], preferred_element_type=F32) + bf_ref[...]
        lf_ref[...] = -_softplus(-f)

    u_ref[...] = jnp.dot(xn_ref[...], w_ref[...], preferred_element_type=F32)


def _inproj(x, g, w, wf, bf, *, tm, tn):
    m, d = x.shape
    nc = w.shape[1]
    return pl.pallas_call(
        _inproj_kernel,
        grid=(m // tm, nc // tn),
        in_specs=[
            pl.BlockSpec((tm, d), lambda i, j: (i, 0)),
            pl.BlockSpec((1, d), lambda i, j: (0, 0)),
            pl.BlockSpec((d, tn), lambda i, j: (0, j)),
            pl.BlockSpec((d, LANES), lambda i, j: (0, 0)),
            pl.BlockSpec((1, LANES), lambda i, j: (0, 0)),
        ],
        out_specs=[
            pl.BlockSpec((tm, tn), lambda i, j: (i, j)),
            pl.BlockSpec((tm, LANES), lambda i, j: (i, 0)),
        ],
        out_shape=[jax.ShapeDtypeStruct((m, nc), F32), jax.ShapeDtypeStruct((m, LANES), F32)],
        scratch_shapes=[pltpu.VMEM((tm, d), BF16)],
        compiler_params=_cparams(("arbitrary", "arbitrary")),
        name="inproj",
    )(x, g, w, wf, bf)


def _prev_rows(xp_s, x, first_rows, states, *, rows, seq_len, multi_seq, r0=0):
    if r0 == 0:
        if multi_seq:
            xp_s[0:SUBLANES, :] = jnp.zeros((SUBLANES, x.shape[1]), F32)
        else:
            @pl.when(pl.program_id(0) == 0)
            def _():
                xp_s[0:SUBLANES, :] = first_rows[...]
    xp_s[SUBLANES + r0:SUBLANES + r0 + rows, :] = x
    prevs = []
    for k in range(1, len(states) + 1):
        pk = xp_s[pl.ds(SUBLANES + r0 - k, rows), :]
        if multi_seq:
            pos = (r0 + lax.broadcasted_iota(jnp.int32, (rows, 1), 0)) % seq_len
            pk = jnp.where(pos >= k, pk, states[k - 1][r0:r0 + rows, :])
        prevs.append(pk)
    return prevs


def _rglru_kernel(xr_ref, yg_ref, first_ref, p1_ref, p2_ref, p3_ref, h0_ref, wc_ref, bc_ref, wg_ref, ba_ref,
                  bx_ref, lam_ref, go_ref, o_ref, hl_ref, xp_s, a_s, b_s, h_s, *, tr, seq_len, n_blocks,
                  multi_seq):
    i = pl.program_id(0)
    d = xr_ref.shape[1]
    blk = d // n_blocks
    x = xr_ref[...]

    @pl.when(i == 0)
    def _():
        h_s[...] = jnp.zeros_like(h_s)

    prev1, prev2, prev3 = _prev_rows(xp_s, x, first_ref, (p1_ref, p2_ref, p3_ref), rows=tr, seq_len=seq_len,
                                     multi_seq=multi_seq)
    xc = bc_ref[...] + wc_ref[0:1, :] * prev3
    xc = xc + wc_ref[1:2, :] * prev2
    xc = xc + wc_ref[2:3, :] * prev1
    xc = xc + wc_ref[3:4, :] * x
    xp_s[0:SUBLANES, :] = x[tr - SUBLANES:, :]

    sp = _softplus(-lam_ref[...])
    xcb = xc.astype(BF16)
    for n in range(n_blocks):
        sl = slice(n * blk, (n + 1) * blk)
        gn = jnp.dot(xcb[:, sl], wg_ref[n], preferred_element_type=F32)
        r = jax.nn.sigmoid(gn[:, :blk] + ba_ref[:, sl])
        ig = jax.nn.sigmoid(gn[:, blk:] + bx_ref[:, sl])
        log_a = (-LRU_C) * r * sp[:, sl]
        a = jnp.exp(log_a)
        a_s[:, sl] = a
        b_s[:, sl] = jnp.sqrt(-jnp.tanh(log_a) * (a * a + 1.0)) * (ig * xc[:, sl])

    row = lax.broadcasted_iota(jnp.int32, (SUBLANES, d), 0)
    groups_per_seq = seq_len // SUBLANES

    def group(gi, h):
        r0 = pl.multiple_of(gi * SUBLANES, SUBLANES)
        gg = i * (tr // SUBLANES) + gi
        seq = gg // groups_per_seq
        first = (gg % groups_per_seq) == 0
        h = jnp.where(first, h0_ref[pl.ds(seq, 1), :], h)
        a8 = a_s[pl.ds(r0, SUBLANES), :]
        b8 = b_s[pl.ds(r0, SUBLANES), :]
        for sh in (1, 2, 4):
            ok = row >= sh
            b8 = jnp.where(ok, a8 * pltpu.roll(b8, sh, axis=0) + b8, b8)
            a8 = jnp.where(ok, a8 * pltpu.roll(a8, sh, axis=0), a8)
        h8 = a8 * h + b8
        b_s[pl.ds(r0, SUBLANES), :] = h8
        hn = h8[SUBLANES - 1:SUBLANES, :]

        @pl.when((gg % groups_per_seq) == groups_per_seq - 1)
        def _():
            hl_ref[pl.ds(seq, 1), :] = hn
        return hn

    h_s[...] = lax.fori_loop(0, tr // SUBLANES, group, h_s[...])

    o = b_s[...] * _gelu(yg_ref[...])
    o_ref[...] = _rms(o, go_ref[...]).astype(BF16)


def _conv_state_inputs(state, *, rows, seq_len, multi_seq):
    n_seq, w, c = state.shape
    dummy = jnp.zeros((SUBLANES, c), F32)
    if multi_seq:
        assert rows == n_seq * seq_len
        first = dummy
        states = [_place_state(state, seq_len, k) for k in range(1, w + 1)]
    else:
        assert n_seq == 1
        first = jnp.pad(state[0], ((SUBLANES - w, 0), (0, 0)))
        states = [dummy] * w
    return first, states


def _rglru(u, rnn_state, h0, wc, bc, wg, ba, bx, lam, go, *, tr, seq_len, d_rnn):
    m = u.shape[0]
    n_seq = m // seq_len
    multi_seq = n_seq > 1
    n_blocks = wg.shape[0]
    first, states = _conv_state_inputs(rnn_state, rows=tr, seq_len=seq_len, multi_seq=multi_seq)
    row_spec = lambda c: pl.BlockSpec((tr, d_rnn), lambda i: (i, c))
    full = lambda a: pl.BlockSpec(a.shape, lambda i: (0,) * a.ndim)
    kern = functools.partial(_rglru_kernel, tr=tr, seq_len=seq_len, n_blocks=n_blocks, multi_seq=multi_seq)
    return pl.pallas_call(
        kern,
        grid=(m // tr,),
        in_specs=[row_spec(0), row_spec(1), full(first)] + [full(s) for s in states] + [
            full(h0), full(wc), full(bc), full(wg), full(ba), full(bx), full(lam), full(go)],
        out_specs=[pl.BlockSpec((tr, d_rnn), lambda i: (i, 0)), pl.BlockSpec((n_seq, d_rnn), lambda i: (0, 0))],
        out_shape=[jax.ShapeDtypeStruct((m, d_rnn), BF16), jax.ShapeDtypeStruct((n_seq, d_rnn), F32)],
        scratch_shapes=[pltpu.VMEM((tr + SUBLANES, d_rnn), F32), pltpu.VMEM((tr, d_rnn), F32),
                        pltpu.VMEM((tr, d_rnn), F32), pltpu.VMEM((1, d_rnn), F32)],
        compiler_params=_cparams(("arbitrary",)),
        name="rglru",
    )(u, u, first, *states, h0, wc, bc, wg, ba, bx, lam, go)


def _prep_kernel(q_ref, k_ref, v_ref, lf_ref, c0_ref, qb_ref, kb_ref, vb_ref, cq_ref, ckt_ref, carry_s,
                 *, tm, n_kv, group):
    @pl.when(pl.program_id(0) == 0)
    def _():
        carry_s[...] = c0_ref[...]

    qb_ref[...] = (q_ref[...] * (HEAD_DIM ** -0.5)).astype(BF16)
    kb_ref[...] = k_ref[...].astype(BF16)
    vb_ref[...] = v_ref[...].astype(BF16)
    c = _split3_dot(_lower_tri(tm), lf_ref[...]) + carry_s[...]
    carry_s[...] = c[tm - 1:tm, :]
    ct = c.T
    for g in range(n_kv):
        h0 = g * group
        a0 = h0 // SUBLANES * SUBLANES
        rows8 = ct[a0:a0 + SUBLANES, :]
        ckt_ref[g] = rows8 if h0 == a0 else pltpu.roll(rows8, SUBLANES - (h0 - a0), axis=0)
        cq_ref[g] = c if g == 0 else pltpu.roll(c, LANES - h0, axis=1)


def _prep(u, lf, c0, *, tm, col_q, d_attn, n_kv, group):
    m = u.shape[0]
    dkv = n_kv * HEAD_DIM
    cq_blk, ck_blk, cv_blk = col_q // d_attn, (col_q + d_attn) // dkv, (col_q + d_attn + dkv) // dkv
    kern = functools.partial(_prep_kernel, tm=tm, n_kv=n_kv, group=group)
    return pl.pallas_call(
        kern,
        grid=(m // tm,),
        in_specs=[
            pl.BlockSpec((tm, d_attn), lambda i: (i, cq_blk)),
            pl.BlockSpec((tm, dkv), lambda i: (i, ck_blk)),
            pl.BlockSpec((tm, dkv), lambda i: (i, cv_blk)),
            pl.BlockSpec((tm, LANES), lambda i: (i, 0)),
            pl.BlockSpec((1, LANES), lambda i: (0, 0)),
        ],
        out_specs=[
            pl.BlockSpec((tm, d_attn), lambda i: (i, 0)),
            pl.BlockSpec((tm, dkv), lambda i: (i, 0)),
            pl.BlockSpec((tm, dkv), lambda i: (i, 0)),
            pl.BlockSpec((n_kv, tm, LANES), lambda i: (0, i, 0)),
            pl.BlockSpec((n_kv, SUBLANES, tm), lambda i: (0, 0, i)),
        ],
        out_shape=[
            jax.ShapeDtypeStruct((m, d_attn), BF16),
            jax.ShapeDtypeStruct((m, dkv), BF16),
            jax.ShapeDtypeStruct((m, dkv), BF16),
            jax.ShapeDtypeStruct((n_kv, m, LANES), F32),
            jax.ShapeDtypeStruct((n_kv, SUBLANES, m), F32),
        ],
        scratch_shapes=[pltpu.VMEM((1, LANES), F32)],
        compiler_params=_cparams(("arbitrary",)),
        name="attn_prep",
    )(u, u, u, lf, c0)


def _online_update(r, s, v, m_s, l_s, acc_s):
    m_prev = m_s[r]
    m_new = jnp.maximum(m_prev, jnp.max(s, axis=1, keepdims=True))
    alpha = jnp.exp(m_prev - m_new)
    p = jnp.exp(s - m_new)
    l_s[r] = alpha * l_s[r] + jnp.sum(p, axis=1, keepdims=True)
    acc_s[r] = alpha * acc_s[r] + jnp.dot(p.astype(BF16), v, preferred_element_type=F32)
    m_s[r] = m_new


def _qk(q, k):
    return lax.dot_general(q, k, (((1,), (1,)), ((), ())), preferred_element_type=F32)


def _attn_kernel(q_ref, k_ref, v_ref, cq_ref, ck_ref, km_ref, vm_ref, ckm_ref, o_ref, m_s, l_s, acc_s,
                 *, tq, tk, group, n_prefix):
    qi = pl.program_id(1)
    ki = pl.program_id(2)
    cq = cq_ref[...]

    @pl.when(ki == 0)
    def _():
        lane = lax.broadcasted_iota(jnp.int32, (tq, km_ref.shape[0]), 1)
        for r in range(group):
            m_s[r] = jnp.full((tq, 1), NEG_BIG, F32)
            l_s[r] = jnp.zeros((tq, 1), F32)
            acc_s[r] = jnp.zeros((tq, HEAD_DIM), F32)
            s = _qk(q_ref[:, r * HEAD_DIM:(r + 1) * HEAD_DIM], km_ref[...])
            s = s + (cq[:, r:r + 1] - ckm_ref[r:r + 1, :])
            s = jnp.where(lane < n_prefix, s, NEG_BIG)
            _online_update(r, s, vm_ref[...], m_s, l_s, acc_s)

    def step(masked):
        if masked:
            rows = qi * tq + lax.broadcasted_iota(jnp.int32, (tq, tk), 0)
            cols = ki * tk + lax.broadcasted_iota(jnp.int32, (tq, tk), 1)
            keep = cols <= rows
        for r in range(group):
            s = _qk(q_ref[:, r * HEAD_DIM:(r + 1) * HEAD_DIM], k_ref[...])
            s = s + (cq[:, r:r + 1] - ck_ref[r:r + 1, :])
            if masked:
                s = jnp.where(keep, s, NEG_BIG)
            _online_update(r, s, v_ref[...], m_s, l_s, acc_s)

    last = ((qi + 1) * tq - 1) // tk

    @pl.when(ki * tk + tk - 1 <= qi * tq)
    def _():
        step(False)

    @pl.when(jnp.logical_and(ki * tk + tk - 1 > qi * tq, ki <= last))
    def _():
        step(True)

    @pl.when(ki == last)
    def _():
        for r in range(group):
            o_ref[:, r * HEAD_DIM:(r + 1) * HEAD_DIM] = acc_s[r] / l_s[r]


def _attn(qb, kb, vb, cq, ckt, km, vm, ckm, *, tq, tk, n_kv, group, n_prefix):
    m, d_attn = qb.shape
    gw = group * HEAD_DIM
    npad = km.shape[0]

    def kv_map(g, qi, ki):
        return (jnp.minimum(ki, ((qi + 1) * tq - 1) // tk), g)

    def ck_map(g, qi, ki):
        return (g, 0, jnp.minimum(ki, ((qi + 1) * tq - 1) // tk))

    kern = functools.partial(_attn_kernel, tq=tq, tk=tk, group=group, n_prefix=n_prefix)
    return pl.pallas_call(
        kern,
        grid=(n_kv, m // tq, m // tk),
        in_specs=[
            pl.BlockSpec((tq, gw), lambda g, qi, ki: (qi, g)),
            pl.BlockSpec((tk, HEAD_DIM), kv_map),
            pl.BlockSpec((tk, HEAD_DIM), kv_map),
            pl.BlockSpec((None, tq, LANES), lambda g, qi, ki: (g, qi, 0)),
            pl.BlockSpec((None, SUBLANES, tk), ck_map),
            pl.BlockSpec((npad, HEAD_DIM), lambda g, qi, ki: (0, g)),
            pl.BlockSpec((npad, HEAD_DIM), lambda g, qi, ki: (0, g)),
            pl.BlockSpec((None, SUBLANES, npad), lambda g, qi, ki: (g, 0, 0)),
        ],
        out_specs=pl.BlockSpec((tq, gw), lambda g, qi, ki: (qi, g)),
        out_shape=jax.ShapeDtypeStruct((m, d_attn), F32),
        scratch_shapes=[pltpu.VMEM((group, tq, 1), F32), pltpu.VMEM((group, tq, 1), F32),
                        pltpu.VMEM((group, tq, HEAD_DIM), F32)],
        compiler_params=_cparams(("arbitrary", "arbitrary", "arbitrary")),
        name="fox_attn",
    )(qb, kb, vb, cq, ckt, km, vm, ckm)


LOG2E = 1.4426950408889634
AUG = 16
SKIP_LOG2 = 160.0


def _bf16_split3(x):
    hi = x.astype(BF16)
    r1 = x - hi.astype(F32)
    mid = r1.astype(BF16)
    lo = (r1 - mid.astype(F32)).astype(BF16)
    return hi, mid, lo


def _prep2_kernel(q_ref, k_ref, v_ref, lf_ref, c0_ref, qat_ref, ka_ref, vt_ref, st_ref, carry_s,
                  *, tm, n_kv, group):
    @pl.when(pl.program_id(0) == 0)
    def _():
        carry_s[...] = c0_ref[...]

    n_heads = n_kv * group
    c = _split3_dot(_lower_tri(tm), lf_ref[...]) + carry_s[...]
    carry_s[...] = c[tm - 1:tm, :]
    c2 = c * LOG2E
    splits = _bf16_split3(c2)
    splits_t = [s.astype(F32).T for s in splits]

    lane = lax.broadcasted_iota(jnp.int32, (1, LANES), 1)
    sub_aug = lax.broadcasted_iota(jnp.int32, (AUG, tm), 0)
    sel_r = lax.broadcasted_iota(jnp.int32, (LANES, LANES), 0)
    sel_c = lax.broadcasted_iota(jnp.int32, (LANES, LANES), 1)
    qstat = jnp.zeros((1, LANES), F32)
    kstat = jnp.zeros((1, LANES), F32)
    ones_cols = jnp.where(jnp.logical_and(lane >= 3 * group, lane < 3 * group + 3), 1.0, 0.0)

    for g in range(n_kv):
        kb = k_ref[:, g * HEAD_DIM:(g + 1) * HEAD_DIM].astype(BF16)
        ka_ref[g, 0, :, :HEAD_DIM] = kb
        aug = jnp.broadcast_to(ones_cols, (tm, LANES))
        for j in range(3):
            pick = jnp.logical_and(sel_r // group == g, sel_c == 3 * (sel_r % group) + j)
            aug = aug + jnp.dot(splits[j], jnp.where(pick, 1.0, 0.0).astype(BF16), preferred_element_type=F32)
        ka_ref[g, 0, :, HEAD_DIM:] = aug.astype(BF16)
        vt_ref[g, 0] = v_ref[:, g * HEAD_DIM:(g + 1) * HEAD_DIM].T.astype(BF16)
        kf = kb.astype(F32)
        kmax = jnp.max(jnp.sum(kf * kf, axis=1, keepdims=True), axis=0, keepdims=True)
        kstat = jnp.where(lane == g, kmax, kstat)
        for r in range(group):
            h = g * group + r
            qb = (q_ref[:, h * HEAD_DIM:(h + 1) * HEAD_DIM] * (HEAD_DIM ** -0.5 * LOG2E)).astype(BF16)
            qf = qb.astype(F32)
            qmax = jnp.max(jnp.sum(qf * qf, axis=1, keepdims=True), axis=0, keepdims=True)
            qstat = jnp.where(lane == h, qmax, qstat)
            qat_ref[h, :HEAD_DIM, :] = qf.T.astype(BF16)
            rows = jnp.where(jnp.logical_and(sub_aug >= 3 * r, sub_aug < 3 * r + 3), -1.0, 0.0)
            for j in range(3):
                rows = jnp.where(sub_aug == 3 * group + j, splits_t[j][h:h + 1, :], rows)
            qat_ref[h, HEAD_DIM:HEAD_DIM + AUG, :] = rows.astype(BF16)
            qat_ref[h, HEAD_DIM + AUG:, :] = jnp.zeros((HEAD_DIM - AUG, tm), BF16)
    del n_heads
    st_ref[0, 0:1, :] = qstat
    st_ref[0, 1:2, :] = kstat
    st_ref[0, 2:3, :] = c2[tm - 1:tm, :]
    st_ref[0, 3:4, :] = c2[0:1, :]
    st_ref[0, 4:, :] = jnp.zeros((SUBLANES - 4, LANES), F32)


def _prep2(u, lf, c0, *, tm, col_q, d_attn, n_kv, group):
    m = u.shape[0]
    nt = m // tm
    n_heads = n_kv * group
    dkv = n_kv * HEAD_DIM
    assert 3 * group + 3 <= AUG
    cq_blk, ck_blk, cv_blk = col_q // d_attn, (col_q + d_attn) // dkv, (col_q + d_attn + dkv) // dkv
    kern = functools.partial(_prep2_kernel, tm=tm, n_kv=n_kv, group=group)
    return pl.pallas_call(
        kern,
        grid=(nt,),
        in_specs=[
            pl.BlockSpec((tm, d_attn), lambda i: (i, cq_blk)),
            pl.BlockSpec((tm, dkv), lambda i: (i, ck_blk)),
            pl.BlockSpec((tm, dkv), lambda i: (i, cv_blk)),
            pl.BlockSpec((tm, LANES), lambda i: (i, 0)),
            pl.BlockSpec((1, LANES), lambda i: (0, 0)),
        ],
        out_specs=[
            pl.BlockSpec((n_heads, 2 * HEAD_DIM, tm), lambda i: (0, 0, i)),
            pl.BlockSpec((n_kv, 1, tm, 2 * HEAD_DIM), lambda i: (0, i, 0, 0)),
            pl.BlockSpec((n_kv, 1, HEAD_DIM, tm), lambda i: (0, i, 0, 0)),
            pl.BlockSpec((1, SUBLANES, LANES), lambda i: (i, 0, 0)),
        ],
        out_shape=[
            jax.ShapeDtypeStruct((n_heads, 2 * HEAD_DIM, m), BF16),
            jax.ShapeDtypeStruct((n_kv, nt, tm, 2 * HEAD_DIM), BF16),
            jax.ShapeDtypeStruct((n_kv, nt, HEAD_DIM, tm), BF16),
            jax.ShapeDtypeStruct((nt, SUBLANES, LANES), F32),
        ],
        scratch_shapes=[pltpu.VMEM((1, LANES), F32)],
        compiler_params=_cparams(("arbitrary",)),
        name="attn_prep",
    )(u, u, u, lf, c0)


def _attn2_kernel(nt_ref, qat_ref, ka_ref, vt_ref, kam_ref, vtm_ref, o_ref, sa_buf, sb_buf, *, t, nq):
    h = pl.program_id(0)
    qi = pl.program_id(1)
    q = qat_ref[...]

    def scores(ka):
        return jnp.dot(ka, q, preferred_element_type=F32)

    def absorb(s, smax, vt, carry):
        m_prev, l_prev, acc = carry
        m_new = jnp.maximum(m_prev, smax)
        alpha = jnp.exp2(m_prev - m_new)
        p = jnp.exp2(s - m_new)
        l_new = alpha * l_prev + jnp.sum(p, axis=0, keepdims=True)
        acc = alpha * acc + jnp.dot(vt, p.astype(BF16), preferred_element_type=F32)
        return m_new, l_new, acc

    carry = (jnp.full((1, t), NEG_BIG, F32), jnp.zeros((1, t), F32), jnp.zeros((HEAD_DIM, t), F32))
    key = lax.broadcasted_iota(jnp.int32, (t, t), 0)
    qry = lax.broadcasted_iota(jnp.int32, (t, t), 1)
    s_diag = jnp.where(key <= qry, scores(ka_ref[qi]), NEG_BIG)
    sa_buf[...] = s_diag
    n_older = nt_ref[h * nq + qi]
    colmax = lambda s: jnp.max(s, axis=0, keepdims=True)

    def step(k_next, src, dst, smax_src, carry):
        s_next = scores(ka_ref[k_next])
        dst[...] = s_next
        return colmax(s_next), absorb(src[...], smax_src, vt_ref[k_next + 1], carry)

    def older_pair(j, state):
        smax_a, carry = state
        k1 = qi - 1 - 2 * j
        smax_b, carry = step(k1, sa_buf, sb_buf, smax_a, carry)
        return step(k1 - 1, sb_buf, sa_buf, smax_b, carry)

    smax_a, carry = lax.fori_loop(0, n_older // 2, older_pair, (colmax(s_diag), carry))
    k_last = qi - n_older

    def odd_tail(carry):
        smax_b, carry = step(k_last, sa_buf, sb_buf, smax_a, carry)
        return absorb(sb_buf[...], smax_b, vt_ref[k_last], carry)

    carry = lax.cond(n_older % 2 == 1, odd_tail, lambda c: absorb(sa_buf[...], smax_a, vt_ref[k_last], c), carry)
    s_pre = scores(kam_ref[...])
    _, l_fin, acc = absorb(s_pre, colmax(s_pre), vtm_ref[...], carry)
    o_ref[...] = (acc / l_fin).T


def _attn2(ntab, qat, ka, vt, kam, vtm, *, t, group):
    n_heads, _, m = qat.shape
    n_kv, nk = ka.shape[:2]
    nq = m // t
    npad = kam.shape[1]
    kern = functools.partial(_attn2_kernel, t=t, nq=nq)
    grid_spec = pltpu.PrefetchScalarGridSpec(
        num_scalar_prefetch=1,
        grid=(n_heads, nq),
        in_specs=[
            pl.BlockSpec((None, 2 * HEAD_DIM, t), lambda h, qi, nt: (h, 0, qi)),
            pl.BlockSpec((None, nk, t, 2 * HEAD_DIM), lambda h, qi, nt: (h // group, 0, 0, 0)),
            pl.BlockSpec((None, nk, HEAD_DIM, t), lambda h, qi, nt: (h // group, 0, 0, 0)),
            pl.BlockSpec((None, npad, 2 * HEAD_DIM), lambda h, qi, nt: (h // group, 0, 0)),
            pl.BlockSpec((None, HEAD_DIM, npad), lambda h, qi, nt: (h // group, 0, 0)),
        ],
        out_specs=pl.BlockSpec((t, HEAD_DIM), lambda h, qi, nt: (qi, h)),
        scratch_shapes=[pltpu.VMEM((t, t), F32), pltpu.VMEM((t, t), F32)],
    )
    return pl.pallas_call(
        kern,
        grid_spec=grid_spec,
        out_shape=jax.ShapeDtypeStruct((m, n_heads * HEAD_DIM), F32),
        compiler_params=_cparams(("arbitrary", "arbitrary")),
        name="fox_attn",
    )(ntab, qat, ka, vt, kam, vtm)


def _older_tile_counts(stats, *, n_kv, group):
    n_heads = n_kv * group
    nt = stats.shape[0]
    qmax = jnp.sqrt(stats[:, 0, :n_heads])
    kmax = jnp.sqrt(jnp.max(stats[:, 1, :n_kv], axis=0))
    smax = qmax * jnp.repeat(kmax, group)[None, :]
    c_last = stats[:, 2, :n_heads]
    c_first = stats[:, 3, :n_heads]
    bias_max = c_first[:, None, :] - c_last[None, :, :]
    needed = 2.0 * smax[:, None, :] + bias_max > -SKIP_LOG2
    qi = jnp.arange(nt)[:, None, None]
    kt = jnp.arange(nt)[None, :, None]
    dist = jnp.where(jnp.logical_and(needed, kt < qi), qi - kt, 0)
    return jnp.max(dist, axis=1).T.astype(jnp.int32).reshape(-1)


def _attn_small_kernel(q_ref, k_ref, v_ref, lf_ref, ck_ref, cv_ref, clf_ref, o_ref, c_ref, ccum_s,
                       *, n_kv, group, n_cached_seqs, p_len, s_len):
    b = pl.program_id(0)
    has_cache = b < n_cached_seqs
    n_blk = p_len // LANES
    tri = _lower_tri(LANES)

    def cblock(j, off):
        r0 = pl.multiple_of(j * LANES, LANES)
        cb = _split3_dot(tri, clf_ref[pl.ds(r0, LANES), :]) + off
        ccum_s[pl.ds(r0, LANES), :] = cb
        return cb[LANES - 1:LANES, :]

    total = lax.fori_loop(0, n_blk, cblock, jnp.zeros((1, LANES), F32))
    total = jnp.where(has_cache, total, 0.0)
    zrows = jnp.zeros((LANES - s_len, LANES), F32)
    c_pad = _split3_dot(tri, jnp.concatenate([lf_ref[...], zrows], axis=0)) + total
    c_own = c_pad[:s_len, :]
    c_ref[...] = c_own
    c_own_t = c_pad.T
    c_cache_t = ccum_s[...].T

    rows = lax.broadcasted_iota(jnp.int32, (s_len, LANES), 0)
    cols = lax.broadcasted_iota(jnp.int32, (s_len, LANES), 1)
    scale = HEAD_DIM ** -0.5
    for g in range(n_kv):
        ksl = slice(g * HEAD_DIM, (g + 1) * HEAD_DIM)
        k_own = jnp.concatenate([k_ref[:, ksl], zrows], axis=0).astype(BF16)
        v_own = jnp.concatenate([v_ref[:, ksl], zrows], axis=0).astype(BF16)
        k_cache = ck_ref[:, ksl].astype(BF16)
        v_cache = cv_ref[:, ksl].astype(BF16)
        for r in range(group):
            h = g * group + r
            q = (q_ref[:, h * HEAD_DIM:(h + 1) * HEAD_DIM] * scale).astype(BF16)
            cqh = c_own[:, h:h + 1]
            s_own = _qk(q, k_own) + (cqh - c_own_t[h:h + 1, :])
            s_own = jnp.where(cols <= rows, s_own, NEG_BIG)
            s_cache = _qk(q, k_cache) + (cqh - c_cache_t[h:h + 1, :])
            s_cache = jnp.where(has_cache, s_cache, NEG_BIG)
            mx = jnp.maximum(jnp.max(s_own, axis=1, keepdims=True), jnp.max(s_cache, axis=1, keepdims=True))
            p_own = jnp.exp(s_own - mx)
            p_cache = jnp.exp(s_cache - mx)
            den = jnp.sum(p_own, axis=1, keepdims=True) + jnp.sum(p_cache, axis=1, keepdims=True)
            num = jnp.dot(p_own.astype(BF16), v_own, preferred_element_type=F32)
            num = num + jnp.dot(p_cache.astype(BF16), v_cache, preferred_element_type=F32)
            o_ref[:, h * HEAD_DIM:(h + 1) * HEAD_DIM] = num / den


def _attn_small(u, lf, cache_k, cache_v, cache_lf, *, n_seq, s_len, col_q, d_attn, n_kv, group):
    n_cached, p_len, dkv = cache_k.shape
    ck_blk, cv_blk = (col_q + d_attn) // dkv, (col_q + d_attn + dkv) // dkv
    cq_blk = col_q // d_attn
    cmap = lambda b: (jnp.minimum(b, n_cached - 1), 0, 0)
    kern = functools.partial(_attn_small_kernel, n_kv=n_kv, group=group, n_cached_seqs=n_cached, p_len=p_len,
                             s_len=s_len)
    return pl.pallas_call(
        kern,
        grid=(n_seq,),
        in_specs=[
            pl.BlockSpec((s_len, d_attn), lambda b: (b, cq_blk)),
            pl.BlockSpec((s_len, dkv), lambda b: (b, ck_blk)),
            pl.BlockSpec((s_len, dkv), lambda b: (b, cv_blk)),
            pl.BlockSpec((s_len, LANES), lambda b: (b, 0)),
            pl.BlockSpec((None, p_len, dkv), cmap),
            pl.BlockSpec((None, p_len, dkv), cmap),
            pl.BlockSpec((None, p_len, LANES), cmap),
        ],
        out_specs=[pl.BlockSpec((s_len, d_attn), lambda b: (b, 0)),
                   pl.BlockSpec((s_len, LANES), lambda b: (b, 0))],
        out_shape=[jax.ShapeDtypeStruct((n_seq * s_len, d_attn), F32),
                   jax.ShapeDtypeStruct((n_seq * s_len, LANES), F32)],
        scratch_shapes=[pltpu.VMEM((p_len, LANES), F32)],
        compiler_params=_cparams(("arbitrary",)),
        name="fox_attn_small",
    )(u, u, u, lf, cache_k, cache_v, cache_lf)


def _outproj_kernel(orn_ref, oat_ref, ga_ref, w_ref, x_ref, h_ref, a_s, *, d_rnn):
    @pl.when(pl.program_id(1) == 0)
    def _():
        a_s[:, :d_rnn] = orn_ref[...]
        a_s[:, d_rnn:] = _rms(oat_ref[...], ga_ref[...]).astype(BF16)

    h_ref[...] = x_ref[...] + jnp.dot(a_s[...], w_ref[...], preferred_element_type=F32)


def _outproj(orn, oat, ga, w, x, *, tm, tn):
    m, d_rnn = orn.shape
    d_attn = oat.shape[1]
    d = x.shape[1]
    return pl.pallas_call(
        functools.partial(_outproj_kernel, d_rnn=d_rnn),
        grid=(m // tm, d // tn),
        in_specs=[
            pl.BlockSpec((tm, d_rnn), lambda i, j: (i, 0)),
            pl.BlockSpec((tm, d_attn), lambda i, j: (i, 0)),
            pl.BlockSpec((1, d_attn), lambda i, j: (0, 0)),
            pl.BlockSpec((d_rnn + d_attn, tn), lambda i, j: (0, j)),
            pl.BlockSpec((tm, tn), lambda i, j: (i, j)),
        ],
        out_specs=pl.BlockSpec((tm, tn), lambda i, j: (i, j)),
        out_shape=jax.ShapeDtypeStruct((m, d), F32),
        scratch_shapes=[pltpu.VMEM((tm, d_rnn + d_attn), BF16)],
        compiler_params=_cparams(("arbitrary", "arbitrary")),
        name="outproj",
    )(orn, oat, ga, w, x)


def _ffn_in_kernel(h_ref, g_ref, wg_ref, wv_ref, wc_ref, bc_ref, first_ref, p1_ref, p2_ref, act_ref, zt_ref,
                   xn_s, zp_s, carry_s, *, tm, seq_len, multi_seq, n_sub):
    i = pl.program_id(0)
    f = pl.program_id(1)

    @pl.when(f == 0)
    def _():
        xn_s[...] = _rms(h_ref[...], g_ref[...]).astype(BF16)

    if not multi_seq:
        @pl.when(i > 0)
        def _():
            zp_s[0:SUBLANES, :] = carry_s[f]

    rs = tm // n_sub
    for r0 in range(0, tm, rs):
        xn = xn_s[r0:r0 + rs, :]
        zg = jnp.dot(xn, wg_ref[...], preferred_element_type=F32)
        zv = jnp.dot(xn, wv_ref[...], preferred_element_type=F32)
        prev1, prev2 = _prev_rows(zp_s, zg, first_ref, (p1_ref, p2_ref), rows=rs, seq_len=seq_len,
                                  multi_seq=multi_seq, r0=r0)
        gate = bc_ref[...] + wc_ref[0:1, :] * prev2
        gate = gate + wc_ref[1:2, :] * prev1
        gate = gate + wc_ref[2:3, :] * zg
        if multi_seq:
            zt_ref[r0:r0 + rs, :] = zg
        elif r0 + rs == tm:
            tail = zg[rs - SUBLANES:, :]
            carry_s[f] = tail
            zt_ref[...] = tail
        act_ref[r0:r0 + rs, :] = (_gelu(gate) * zv).astype(BF16)


def _ffn_in(h, g, w, wc, bc, ffn_state, *, tm, tf, seq_len):
    m, d = h.shape
    d_ff = w.shape[1] // 2
    nf = d_ff // tf
    multi_seq = m // seq_len > 1
    first, states = _conv_state_inputs(ffn_state, rows=tm, seq_len=seq_len, multi_seq=multi_seq)
    zt_rows = tm if multi_seq else SUBLANES
    st_spec = lambda a: pl.BlockSpec((a.shape[0], tf), lambda i, f: (0, f))
    return pl.pallas_call(
        functools.partial(_ffn_in_kernel, tm=tm, seq_len=seq_len, multi_seq=multi_seq,
                          n_sub=2 if tm % (4 * LANES) == 0 else 1),
        grid=(m // tm, nf),
        in_specs=[
            pl.BlockSpec((tm, d), lambda i, f: (i, 0)),
            pl.BlockSpec((1, d), lambda i, f: (0, 0)),
            pl.BlockSpec((d, tf), lambda i, f: (0, f)),
            pl.BlockSpec((d, tf), lambda i, f: (0, f + nf)),
            pl.BlockSpec((wc.shape[0], tf), lambda i, f: (0, f)),
            pl.BlockSpec((1, tf), lambda i, f: (0, f)),
            st_spec(first)] + [st_spec(s) for s in states],
        out_specs=[pl.BlockSpec((tm, tf), lambda i, f: (i, f)),
                   pl.BlockSpec((zt_rows, tf), lambda i, f: (i, f))],
        out_shape=[jax.ShapeDtypeStruct((m, d_ff), BF16),
                   jax.ShapeDtypeStruct((m // tm * zt_rows, d_ff), F32)],
        scratch_shapes=[pltpu.VMEM((tm, d), BF16), pltpu.VMEM((tm + SUBLANES, tf), F32),
                        pltpu.VMEM((nf, SUBLANES, tf), F32)],
        compiler_params=_cparams(("arbitrary", "arbitrary")),
        name="ffn_in",
    )(h, g, w, w, wc, bc, first, *states)


def _ffn_out_kernel(act_ref, w_ref, h_ref, g_ref, y_ref, *, tc):
    k = pl.program_id(1)
    d = y_ref.shape[1]
    chunks = [slice(c, c + tc) for c in range(0, d, tc)]

    @pl.when(k == 0)
    def _():
        y_ref[...] = h_ref[...]

    act = act_ref[...]
    for sl in chunks:
        y_ref[:, sl] += jnp.dot(act, w_ref[:, sl], preferred_element_type=F32)

    @pl.when(k == pl.num_programs(1) - 1)
    def _():
        ssq = jnp.zeros((y_ref.shape[0], 1), F32)
        for sl in chunks:
            yc = y_ref[:, sl]
            ssq = ssq + jnp.sum(yc * yc, axis=-1, keepdims=True)
        inv = lax.rsqrt(ssq / d + EPS)
        for sl in chunks:
            y_ref[:, sl] = y_ref[:, sl] * inv * g_ref[:, sl]


def _ffn_out(act, w, h, g, *, tm, tk):
    m, d_ff = act.shape
    d = h.shape[1]
    return pl.pallas_call(
        functools.partial(_ffn_out_kernel, tc=_pick(d, 1024)),
        grid=(m // tm, d_ff // tk),
        in_specs=[
            pl.BlockSpec((tm, tk), lambda i, k: (i, k)),
            pl.BlockSpec((tk, d), lambda i, k: (k, 0)),
            pl.BlockSpec((tm, d), lambda i, k: (i, 0)),
            pl.BlockSpec((1, d), lambda i, k: (0, 0)),
        ],
        out_specs=pl.BlockSpec((tm, d), lambda i, k: (i, 0)),
        out_shape=jax.ShapeDtypeStruct((m, d), F32),
        compiler_params=_cparams(("arbitrary", "arbitrary")),
        name="ffn_out",
    )(act, w, h, g)


def _pick(n, pref):
    t = min(n, pref)
    while n % t:
        t //= 2
    return t


def _place_state(state, seq_len, back):
    n_seq, w, c = state.shape
    out = jnp.zeros((n_seq, seq_len, c), state.dtype)
    out = out.at[:, :back, :].set(state[:, w - back:, :])
    return out.reshape(n_seq * seq_len, c)


def kernel(x_prompt, x_sample, cache_k, cache_v, cache_logf, state_rnn_h, state_rnn_conv, state_ffn_conv,
           meta_tokens, g_mix, w_in, b_f, w_rnn_conv, b_rnn_conv, w_rg_a, b_rg_a, w_rg_x, b_rg_x, lru_lambda,
           g_out_rnn, g_out_attn, w_out, g_ffn, w_ffn_in, w_ffn_conv, b_ffn_conv, w_ffn_out, g_final):
    depth = g_mix.shape[0]
    assert depth == 1, "single-layer stack"
    batch, seq, d_model = x_prompt.shape
    assert batch == 1
    dec_batch, dec_seq, _ = x_sample.shape
    n_meta = meta_tokens.shape[0]
    assert n_meta == dec_seq, "meta prefix is run as one more sample-length sequence"
    n_heads = b_f.shape[1]
    n_kv = cache_k.shape[3]
    group = n_heads // n_kv
    assert group <= SUBLANES and cache_k.shape[4] == HEAD_DIM
    d_attn = n_heads * HEAD_DIM
    d_rnn = state_rnn_h.shape[2]
    dkv = n_kv * HEAD_DIM
    d_ff = state_ffn_conv.shape[3]
    past = cache_k.shape[2]
    col_q = 2 * d_rnn
    n_main_cols = col_q + d_attn + 2 * dkv
    n_rnn_blocks = w_rg_a.shape[1]
    rnn_w = w_rnn_conv.shape[1]
    ffn_w = w_ffn_conv.shape[1]
    assert rnn_w == 4 and ffn_w == 3

    w_in_b = w_in[0][:, :n_main_cols].astype(BF16)
    wf_b = jnp.pad(w_in[0][:, n_main_cols:], ((0, 0), (0, LANES - n_heads))).astype(BF16)
    bf_p = jnp.pad(b_f[0], (0, LANES - n_heads)).reshape(1, LANES)
    w_out_b = w_out[0].astype(BF16)
    w_ffn_in_b = w_ffn_in[0].astype(BF16)
    w_ffn_out_b = w_ffn_out[0].astype(BF16)
    wg = jnp.concatenate([w_rg_a[0], w_rg_x[0]], axis=-1).astype(BF16)
    row = lambda a: a.reshape(1, -1)
    rg_args = (w_rnn_conv[0], row(b_rnn_conv[0]), wg, row(b_rg_a[0]), row(b_rg_x[0]), row(lru_lambda[0]),
               row(g_out_rnn[0]))

    def chain(x, seq_len, rnn_state, h0, ffn_state, attn_fn, tm, tr, tn, tf, tk):
        u, lf = _inproj(x, row(g_mix[0]), w_in_b, wf_b, bf_p, tm=tm, tn=tn)
        orn, h_last = _rglru(u, rnn_state, h0, *rg_args, tr=tr, seq_len=seq_len, d_rnn=d_rnn)
        oat, attn_aux = attn_fn(u, lf)
        h1 = _outproj(orn, oat, row(g_out_attn[0]), w_out_b, x, tm=tm, tn=_pick(d_model, tn))
        act, zt = _ffn_in(h1, row(g_ffn[0]), w_ffn_in_b, w_ffn_conv[0], row(b_ffn_conv[0]), ffn_state,
                          tm=tm, tf=tf, seq_len=seq_len)
        y = _ffn_out(act, w_ffn_out_b, h1, row(g_final), tm=tm, tk=tk)
        return u, lf, h_last, zt, y, attn_aux

    n_small = dec_batch + 1
    ms = n_small * dec_seq
    xs = jnp.concatenate([x_sample.reshape(dec_batch * dec_seq, d_model), meta_tokens.astype(F32)], axis=0)
    zpad = lambda a: jnp.concatenate([a, jnp.zeros((1,) + a.shape[1:], a.dtype)], axis=0)
    clf = jnp.pad(cache_logf[0], ((0, 0), (0, 0), (0, LANES - n_heads)))

    def small_attn(u, lf):
        return _attn_small(u, lf, cache_k[0].reshape(dec_batch, past, dkv), cache_v[0].reshape(dec_batch, past, dkv),
                           clf, n_seq=n_small, s_len=dec_seq, col_q=col_q, d_attn=d_attn, n_kv=n_kv, group=group)

    tn, tf, tk = _pick(n_main_cols, 1024), _pick(d_ff, 512), _pick(d_ff, 512)
    u_s, lf_s, hl_s, zt_s, y_s, c_s = chain(
        xs, dec_seq, zpad(state_rnn_conv[0]), zpad(state_rnn_h[0]), zpad(state_ffn_conv[0]), small_attn,
        tm=ms, tr=ms, tn=tn, tf=tf, tk=tk)

    m0 = dec_batch * dec_seq
    u_meta = u_s[m0:]
    k_meta = u_meta[:, col_q + d_attn:col_q + d_attn + dkv]
    v_meta = u_meta[:, col_q + d_attn + dkv:]
    c_meta = c_s[m0:]
    npad = -(-n_meta // 16) * 16
    c2m = (c_meta[:, :n_heads] * LOG2E).reshape(n_meta, n_kv, group).transpose(1, 0, 2)
    c2m = jnp.pad(c2m, ((0, 0), (0, npad - n_meta), (0, 0)), constant_values=-NEG_BIG)
    bias_cols = jnp.stack(_bf16_split3(c2m), axis=-1).reshape(n_kv, npad, 3 * group)
    bias_cols = jnp.concatenate([bias_cols, jnp.ones((n_kv, npad, 3), BF16),
                                 jnp.zeros((n_kv, npad, HEAD_DIM - 3 * group - 3), BF16)], axis=-1)
    k_meta_p = jnp.pad(k_meta, ((0, npad - n_meta), (0, 0))).reshape(npad, n_kv, HEAD_DIM).transpose(1, 0, 2)
    kam = jnp.concatenate([k_meta_p.astype(BF16), bias_cols], axis=-1)
    vtm = jnp.pad(v_meta, ((0, npad - n_meta), (0, 0))).reshape(npad, n_kv, HEAD_DIM).transpose(1, 2, 0)
    vtm = vtm.astype(BF16)
    tq = _pick(seq, 512)

    def main_attn(u, lf):
        qat, ka, vt, stats = _prep2(u, lf, c_meta[n_meta - 1:], tm=tq, col_q=col_q, d_attn=d_attn, n_kv=n_kv,
                                    group=group)
        ntab = _older_tile_counts(stats, n_kv=n_kv, group=group)
        return _attn2(ntab, qat, ka, vt, kam, vtm, t=tq, group=group), None

    u_m, lf_m, hl_m, zt_m, y_m, _ = chain(
        x_prompt[0], seq, u_meta[None, n_meta - (rnn_w - 1):, :d_rnn], hl_s[dec_batch:],
        zt_s[None, ms - (ffn_w - 1):, :], main_attn,
        tm=_pick(seq, 512), tr=_pick(seq, 256), tn=tn, tf=tf, tk=tk)

    kcols = slice(col_q + d_attn, col_q + d_attn + dkv)
    vcols = slice(col_q + d_attn + dkv, n_main_cols)
    y_prompt = y_m[None]
    y_sample = y_s[:m0].reshape(dec_batch, dec_seq, d_model)
    new_k_prompt = jnp.concatenate([k_meta, u_m[:, kcols]], axis=0).reshape(1, 1, n_meta + seq, n_kv, HEAD_DIM)
    new_v_prompt = jnp.concatenate([v_meta, u_m[:, vcols]], axis=0).reshape(1, 1, n_meta + seq, n_kv, HEAD_DIM)
    new_logf_prompt = jnp.concatenate([lf_s[m0:, :n_heads], lf_m[:, :n_heads]], axis=0)[None, None]
    new_rnn_h_prompt = hl_m[None]
    new_rnn_conv_prompt = u_m[seq - (rnn_w - 1):, :d_rnn][None, None]
    new_ffn_conv_prompt = zt_m[zt_m.shape[0] - (ffn_w - 1):][None, None]
    us3 = u_s[:m0].reshape(dec_batch, dec_seq, n_main_cols)
    new_k_sample = us3[:, :, kcols].reshape(1, dec_batch, dec_seq, n_kv, HEAD_DIM)
    new_v_sample = us3[:, :, vcols].reshape(1, dec_batch, dec_seq, n_kv, HEAD_DIM)
    new_logf_sample = lf_s[:m0, :n_heads].reshape(1, dec_batch, dec_seq, n_heads)
    new_rnn_h_sample = hl_s[:dec_batch][None]
    new_rnn_conv_sample = us3[:, dec_seq - (rnn_w - 1):, :d_rnn][None]
    new_ffn_conv_sample = zt_s[:m0].reshape(dec_batch, dec_seq, d_ff)[:, dec_seq - (ffn_w - 1):][None]
    return (y_prompt, y_sample, new_k_prompt, new_v_prompt, new_logf_prompt, new_rnn_h_prompt,
            new_rnn_conv_prompt, new_ffn_conv_prompt, new_k_sample, new_v_sample, new_logf_sample,
            new_rnn_h_sample, new_rnn_conv_sample, new_ffn_conv_sample)
```

```python
import functools

import jax
import jax.numpy as jnp
from jax import lax
from jax.experimental import pallas as pl
from jax.experimental.pallas import tpu as pltpu

F32 = jnp.float32
BF16 = jnp.bfloat16

EPS = 1e-6
LRU_C = 8.0
HEAD_DIM = 128
LANES = 128
SUBLANES = 8
NEG_BIG = -1e30
VMEM_LIMIT = 56 * 1024 * 1024


def _cparams(sem):
    return pltpu.CompilerParams(dimension_semantics=sem, vmem_limit_bytes=VMEM_LIMIT)


def _rms(x, g):
    return x * lax.rsqrt(jnp.mean(x * x, axis=-1, keepdims=True) + EPS) * g


def _softplus(x):
    return jnp.maximum(x, 0.0) + jnp.log1p(jnp.exp(-jnp.abs(x)))


def _gelu(x):
    return jax.nn.gelu(x, approximate=True)


def _split3_dot(l_bf16, x):
    hi = x.astype(BF16)
    r1 = x - hi.astype(F32)
    mid = r1.astype(BF16)
    lo = (r1 - mid.astype(F32)).astype(BF16)
    out = jnp.dot(l_bf16, hi, preferred_element_type=F32)
    out = out + jnp.dot(l_bf16, mid, preferred_element_type=F32)
    return out + jnp.dot(l_bf16, lo, preferred_element_type=F32)


def _lower_tri(n):
    r = lax.broadcasted_iota(jnp.int32, (n, n), 0)
    c = lax.broadcasted_iota(jnp.int32, (n, n), 1)
    return jnp.where(c <= r, 1.0, 0.0).astype(BF16)


def _inproj_kernel(x_ref, g_ref, w_ref, wf_ref, bf_ref, u_ref, lf_ref, xn_ref):
    @pl.when(pl.program_id(1) == 0)
    def _():
        xb = _rms(x_ref[...], g_ref[...]).astype(BF16)
        xn_ref[...] = xb
        f = jnp.dot(xb, wf_ref[...], preferred_element_type=F32) + bf_ref[...]
        lf_ref[...] = -_softplus(-f)

    u_ref[...] = jnp.dot(xn_ref[...], w_ref[...], preferred_element_type=F32)


def _inproj(x, g, w, wf, bf, *, tm, tn):
    m, d = x.shape
    nc = w.shape[1]
    return pl.pallas_call(
        _inproj_kernel,
        grid=(m // tm, nc // tn),
        in_specs=[
            pl.BlockSpec((tm, d), lambda i, j: (i, 0)),
            pl.BlockSpec((1, d), lambda i, j: (0, 0)),
            pl.BlockSpec((d, tn), lambda i, j: (0, j)),
            pl.BlockSpec((d, LANES), lambda i, j: (0, 0)),
            pl.BlockSpec((1, LANES), lambda i, j: (0, 0)),
        ],
        out_specs=[
            pl.BlockSpec((tm, tn), lambda i, j: (i, j)),
            pl.BlockSpec((tm, LANES), lambda i, j: (i, 0)),
        ],
        out_shape=[jax.ShapeDtypeStruct((m, nc), F32), jax.ShapeDtypeStruct((m, LANES), F32)],
        scratch_shapes=[pltpu.VMEM((tm, d), BF16)],
        compiler_params=_cparams(("arbitrary", "arbitrary")),
        name="inproj",
    )(x, g, w, wf, bf)


def _prev_rows(xp_s, x, first_rows, states, *, rows, seq_len, multi_seq, r0=0):
    if r0 == 0:
        if multi_seq:
            xp_s[0:SUBLANES, :] = jnp.zeros((SUBLANES, x.shape[1]), F32)
        else:
            @pl.when(pl.program_id(0) == 0)
            def _():
                xp_s[0:SUBLANES, :] = first_rows[...]
    xp_s[SUBLANES + r0:SUBLANES + r0 + rows, :] = x
    prevs = []
    for k in range(1, len(states) + 1):
        pk = xp_s[pl.ds(SUBLANES + r0 - k, rows), :]
        if multi_seq:
            pos = (r0 + lax.broadcasted_iota(jnp.int32, (rows, 1), 0)) % seq_len
            pk = jnp.where(pos >= k, pk, states[k - 1][r0:r0 + rows, :])
        prevs.append(pk)
    return prevs


def _rglru_kernel(xr_ref, yg_ref, first_ref, p1_ref, p2_ref, p3_ref, h0_ref, wc_ref, bc_ref, wg_ref, ba_ref,
                  bx_ref, lam_ref, go_ref, o_ref, hl_ref, xp_s, a_s, b_s, h_s, *, tr, seq_len, n_blocks,
                  multi_seq):
    i = pl.program_id(0)
    d = xr_ref.shape[1]
    blk = d // n_blocks
    x = xr_ref[...]

    @pl.when(i == 0)
    def _():
        h_s[...] = jnp.zeros_like(h_s)

    prev1, prev2, prev3 = _prev_rows(xp_s, x, first_ref, (p1_ref, p2_ref, p3_ref), rows=tr, seq_len=seq_len,
                                     multi_seq=multi_seq)
    xc = bc_ref[...] + wc_ref[0:1, :] * prev3
    xc = xc + wc_ref[1:2, :] * prev2
    xc = xc + wc_ref[2:3, :] * prev1
    xc = xc + wc_ref[3:4, :] * x
    xp_s[0:SUBLANES, :] = x[tr - SUBLANES:, :]

    sp = _softplus(-lam_ref[...])
    xcb = xc.astype(BF16)
    for n in range(n_blocks):
        sl = slice(n * blk, (n + 1) * blk)
        gn = jnp.dot(xcb[:, sl], wg_ref[n], preferred_element_type=F32)
        r = jax.nn.sigmoid(gn[:, :blk] + ba_ref[:, sl])
        ig = jax.nn.sigmoid(gn[:, blk:] + bx_ref[:, sl])
        log_a = (-LRU_C) * r * sp[:, sl]
        a = jnp.exp(log_a)
        a_s[:, sl] = a
        b_s[:, sl] = jnp.sqrt(-jnp.tanh(log_a) * (a * a + 1.0)) * (ig * xc[:, sl])

    row = lax.broadcasted_iota(jnp.int32, (SUBLANES, d), 0)
    groups_per_seq = seq_len // SUBLANES

    def group(gi, h):
        r0 = pl.multiple_of(gi * SUBLANES, SUBLANES)
        gg = i * (tr // SUBLANES) + gi
        seq = gg // groups_per_seq
        first = (gg % groups_per_seq) == 0
        h = jnp.where(first, h0_ref[pl.ds(seq, 1), :], h)
        a8 = a_s[pl.ds(r0, SUBLANES), :]
        b8 = b_s[pl.ds(r0, SUBLANES), :]
        for sh in (1, 2, 4):
            ok = row >= sh
            b8 = jnp.where(ok, a8 * pltpu.roll(b8, sh, axis=0) + b8, b8)
            a8 = jnp.where(ok, a8 * pltpu.roll(a8, sh, axis=0), a8)
        h8 = a8 * h + b8
        b_s[pl.ds(r0, SUBLANES), :] = h8
        hn = h8[SUBLANES - 1:SUBLANES, :]

        @pl.when((gg % groups_per_seq) == groups_per_seq - 1)
        def _():
            hl_ref[pl.ds(seq, 1), :] = hn
        return hn

    h_s[...] = lax.fori_loop(0, tr // SUBLANES, group, h_s[...])

    o = b_s[...] * _gelu(yg_ref[...])
    o_ref[...] = _rms(o, go_ref[...]).astype(BF16)


def _conv_state_inputs(state, *, rows, seq_len, multi_seq):
    n_seq, w, c = state.shape
    dummy = jnp.zeros((SUBLANES, c), F32)
    if multi_seq:
        assert rows == n_seq * seq_len
        first = dummy
        states = [_place_state(state, seq_len, k) for k in range(1, w + 1)]
    else:
        assert n_seq == 1
        first = jnp.pad(state[0], ((SUBLANES - w, 0), (0, 0)))
        states = [dummy] * w
    return first, states


def _rglru(u, rnn_state, h0, wc, bc, wg, ba, bx, lam, go, *, tr, seq_len, d_rnn):
    m = u.shape[0]
    n_seq = m // seq_len
    multi_seq = n_seq > 1
    n_blocks = wg.shape[0]
    first, states = _conv_state_inputs(rnn_state, rows=tr, seq_len=seq_len, multi_seq=multi_seq)
    row_spec = lambda c: pl.BlockSpec((tr, d_rnn), lambda i: (i, c))
    full = lambda a: pl.BlockSpec(a.shape, lambda i: (0,) * a.ndim)
    kern = functools.partial(_rglru_kernel, tr=tr, seq_len=seq_len, n_blocks=n_blocks, multi_seq=multi_seq)
    return pl.pallas_call(
        kern,
        grid=(m // tr,),
        in_specs=[row_spec(0), row_spec(1), full(first)] + [full(s) for s in states] + [
            full(h0), full(wc), full(bc), full(wg), full(ba), full(bx), full(lam), full(go)],
        out_specs=[pl.BlockSpec((tr, d_rnn), lambda i: (i, 0)), pl.BlockSpec((n_seq, d_rnn), lambda i: (0, 0))],
        out_shape=[jax.ShapeDtypeStruct((m, d_rnn), BF16), jax.ShapeDtypeStruct((n_seq, d_rnn), F32)],
        scratch_shapes=[pltpu.VMEM((tr + SUBLANES, d_rnn), F32), pltpu.VMEM((tr, d_rnn), F32),
                        pltpu.VMEM((tr, d_rnn), F32), pltpu.VMEM((1, d_rnn), F32)],
        compiler_params=_cparams(("arbitrary",)),
        name="rglru",
    )(u, u, first, *states, h0, wc, bc, wg, ba, bx, lam, go)


def _prep_kernel(q_ref, k_ref, v_ref, lf_ref, c0_ref, qb_ref, kb_ref, vb_ref, cq_ref, ckt_ref, carry_s,
                 *, tm, n_kv, group):
    @pl.when(pl.program_id(0) == 0)
    def _():
        carry_s[...] = c0_ref[...]

    qb_ref[...] = (q_ref[...] * (HEAD_DIM ** -0.5)).astype(BF16)
    kb_ref[...] = k_ref[...].astype(BF16)
    vb_ref[...] = v_ref[...].astype(BF16)
    c = _split3_dot(_lower_tri(tm), lf_ref[...]) + carry_s[...]
    carry_s[...] = c[tm - 1:tm, :]
    ct = c.T
    for g in range(n_kv):
        h0 = g * group
        a0 = h0 // SUBLANES * SUBLANES
        rows8 = ct[a0:a0 + SUBLANES, :]
        ckt_ref[g] = rows8 if h0 == a0 else pltpu.roll(rows8, SUBLANES - (h0 - a0), axis=0)
        cq_ref[g] = c if g == 0 else pltpu.roll(c, LANES - h0, axis=1)


def _prep(u, lf, c0, *, tm, col_q, d_attn, n_kv, group):
    m = u.shape[0]
    dkv = n_kv * HEAD_DIM
    cq_blk, ck_blk, cv_blk = col_q // d_attn, (col_q + d_attn) // dkv, (col_q + d_attn + dkv) // dkv
    kern = functools.partial(_prep_kernel, tm=tm, n_kv=n_kv, group=group)
    return pl.pallas_call(
        kern,
        grid=(m // tm,),
        in_specs=[
            pl.BlockSpec((tm, d_attn), lambda i: (i, cq_blk)),
            pl.BlockSpec((tm, dkv), lambda i: (i, ck_blk)),
            pl.BlockSpec((tm, dkv), lambda i: (i, cv_blk)),
            pl.BlockSpec((tm, LANES), lambda i: (i, 0)),
            pl.BlockSpec((1, LANES), lambda i: (0, 0)),
        ],
        out_specs=[
            pl.BlockSpec((tm, d_attn), lambda i: (i, 0)),
            pl.BlockSpec((tm, dkv), lambda i: (i, 0)),
            pl.BlockSpec((tm, dkv), lambda i: (i, 0)),
            pl.BlockSpec((n_kv, tm, LANES), lambda i: (0, i, 0)),
            pl.BlockSpec((n_kv, SUBLANES, tm), lambda i: (0, 0, i)),
        ],
        out_shape=[
            jax.ShapeDtypeStruct((m, d_attn), BF16),
            jax.ShapeDtypeStruct((m, dkv), BF16),
            jax.ShapeDtypeStruct((m, dkv), BF16),
            jax.ShapeDtypeStruct((n_kv, m, LANES), F32),
            jax.ShapeDtypeStruct((n_kv, SUBLANES, m), F32),
        ],
        scratch_shapes=[pltpu.VMEM((1, LANES), F32)],
        compiler_params=_cparams(("arbitrary",)),
        name="attn_prep",
    )(u, u, u, lf, c0)


def _online_update(r, s, v, m_s, l_s, acc_s):
    m_prev = m_s[r]
    m_new = jnp.maximum(m_prev, jnp.max(s, axis=1, keepdims=True))
    alpha = jnp.exp(m_prev - m_new)
    p = jnp.exp(s - m_new)
    l_s[r] = alpha * l_s[r] + jnp.sum(p, axis=1, keepdims=True)
    acc_s[r] = alpha * acc_s[r] + jnp.dot(p.astype(BF16), v, preferred_element_type=F32)
    m_s[r] = m_new


def _qk(q, k):
    return lax.dot_general(q, k, (((1,), (1,)), ((), ())), preferred_element_type=F32)


def _attn_kernel(q_ref, k_ref, v_ref, cq_ref, ck_ref, km_ref, vm_ref, ckm_ref, o_ref, m_s, l_s, acc_s,
                 *, tq, tk, group, n_prefix):
    qi = pl.program_id(1)
    ki = pl.program_id(2)
    cq = cq_ref[...]

    @pl.when(ki == 0)
    def _():
        lane = lax.broadcasted_iota(jnp.int32, (tq, km_ref.shape[0]), 1)
        for r in range(group):
            m_s[r] = jnp.full((tq, 1), NEG_BIG, F32)
            l_s[r] = jnp.zeros((tq, 1), F32)
            acc_s[r] = jnp.zeros((tq, HEAD_DIM), F32)
            s = _qk(q_ref[:, r * HEAD_DIM:(r + 1) * HEAD_DIM], km_ref[...])
            s = s + (cq[:, r:r + 1] - ckm_ref[r:r + 1, :])
            s = jnp.where(lane < n_prefix, s, NEG_BIG)
            _online_update(r, s, vm_ref[...], m_s, l_s, acc_s)

    def step(masked):
        if masked:
            rows = qi * tq + lax.broadcasted_iota(jnp.int32, (tq, tk), 0)
            cols = ki * tk + lax.broadcasted_iota(jnp.int32, (tq, tk), 1)
            keep = cols <= rows
        for r in range(group):
            s = _qk(q_ref[:, r * HEAD_DIM:(r + 1) * HEAD_DIM], k_ref[...])
            s = s + (cq[:, r:r + 1] - ck_ref[r:r + 1, :])
            if masked:
                s = jnp.where(keep, s, NEG_BIG)
            _online_update(r, s, v_ref[...], m_s, l_s, acc_s)

    last = ((qi + 1) * tq - 1) // tk

    @pl.when(ki * tk + tk - 1 <= qi * tq)
    def _():
        step(False)

    @pl.when(jnp.logical_and(ki * tk + tk - 1 > qi * tq, ki <= last))
    def _():
        step(True)

    @pl.when(ki == last)
    def _():
        for r in range(group):
            o_ref[:, r * HEAD_DIM:(r + 1) * HEAD_DIM] = acc_s[r] / l_s[r]


def _attn(qb, kb, vb, cq, ckt, km, vm, ckm, *, tq, tk, n_kv, group, n_prefix):
    m, d_attn = qb.shape
    gw = group * HEAD_DIM
    npad = km.shape[0]

    def kv_map(g, qi, ki):
        return (jnp.minimum(ki, ((qi + 1) * tq - 1) // tk), g)

    def ck_map(g, qi, ki):
        return (g, 0, jnp.minimum(ki, ((qi + 1) * tq - 1) // tk))

    kern = functools.partial(_attn_kernel, tq=tq, tk=tk, group=group, n_prefix=n_prefix)
    return pl.pallas_call(
        kern,
        grid=(n_kv, m // tq, m // tk),
        in_specs=[
            pl.BlockSpec((tq, gw), lambda g, qi, ki: (qi, g)),
            pl.BlockSpec((tk, HEAD_DIM), kv_map),
            pl.BlockSpec((tk, HEAD_DIM), kv_map),
            pl.BlockSpec((None, tq, LANES), lambda g, qi, ki: (g, qi, 0)),
            pl.BlockSpec((None, SUBLANES, tk), ck_map),
            pl.BlockSpec((npad, HEAD_DIM), lambda g, qi, ki: (0, g)),
            pl.BlockSpec((npad, HEAD_DIM), lambda g, qi, ki: (0, g)),
            pl.BlockSpec((None, SUBLANES, npad), lambda g, qi, ki: (g, 0, 0)),
        ],
        out_specs=pl.BlockSpec((tq, gw), lambda g, qi, ki: (qi, g)),
        out_shape=jax.ShapeDtypeStruct((m, d_attn), F32),
        scratch_shapes=[pltpu.VMEM((group, tq, 1), F32), pltpu.VMEM((group, tq, 1), F32),
                        pltpu.VMEM((group, tq, HEAD_DIM), F32)],
        compiler_params=_cparams(("arbitrary", "arbitrary", "arbitrary")),
        name="fox_attn",
    )(qb, kb, vb, cq, ckt, km, vm, ckm)


LOG2E = 1.4426950408889634
AUG = 16
SKIP_LOG2 = 160.0


def _bf16_split3(x):
    hi = x.astype(BF16)
    r1 = x - hi.astype(F32)
    mid = r1.astype(BF16)
    lo = (r1 - mid.astype(F32)).astype(BF16)
    return hi, mid, lo


def _prep2_kernel(q_ref, k_ref, v_ref, lf_ref, c0_ref, qat_ref, ka_ref, vt_ref, st_ref, carry_s,
                  *, tm, n_kv, group):
    @pl.when(pl.program_id(0) == 0)
    def _():
        carry_s[...] = c0_ref[...]

    n_heads = n_kv * group
    c = _split3_dot(_lower_tri(tm), lf_ref[...]) + carry_s[...]
    carry_s[...] = c[tm - 1:tm, :]
    c2 = c * LOG2E
    splits = _bf16_split3(c2)
    splits_t = [s.astype(F32).T for s in splits]

    lane = lax.broadcasted_iota(jnp.int32, (1, LANES), 1)
    sub_aug = lax.broadcasted_iota(jnp.int32, (AUG, tm), 0)
    sel_r = lax.broadcasted_iota(jnp.int32, (LANES, LANES), 0)
    sel_c = lax.broadcasted_iota(jnp.int32, (LANES, LANES), 1)
    qstat = jnp.zeros((1, LANES), F32)
    kstat = jnp.zeros((1, LANES), F32)
    ones_cols = jnp.where(jnp.logical_and(lane >= 3 * group, lane < 3 * group + 3), 1.0, 0.0)

    for g in range(n_kv):
        kb = k_ref[:, g * HEAD_DIM:(g + 1) * HEAD_DIM].astype(BF16)
        ka_ref[g, 0, :, :HEAD_DIM] = kb
        aug = jnp.broadcast_to(ones_cols, (tm, LANES))
        for j in range(3):
            pick = jnp.logical_and(sel_r // group == g, sel_c == 3 * (sel_r % group) + j)
            aug = aug + jnp.dot(splits[j], jnp.where(pick, 1.0, 0.0).astype(BF16), preferred_element_type=F32)
        ka_ref[g, 0, :, HEAD_DIM:] = aug.astype(BF16)
        vt_ref[g, 0] = v_ref[:, g * HEAD_DIM:(g + 1) * HEAD_DIM].T.astype(BF16)
        kf = kb.astype(F32)
        kmax = jnp.max(jnp.sum(kf * kf, axis=1, keepdims=True), axis=0, keepdims=True)
        kstat = jnp.where(lane == g, kmax, kstat)
        for r in range(group):
            h = g * group + r
            qb = (q_ref[:, h * HEAD_DIM:(h + 1) * HEAD_DIM] * (HEAD_DIM ** -0.5 * LOG2E)).astype(BF16)
            qf = qb.astype(F32)
            qmax = jnp.max(jnp.sum(qf * qf, axis=1, keepdims=True), axis=0, keepdims=True)
            qstat = jnp.where(lane == h, qmax, qstat)
            qat_ref[h, :HEAD_DIM, :] = qf.T.astype(BF16)
            rows = jnp.where(jnp.logical_and(sub_aug >= 3 * r, sub_aug < 3 * r + 3), -1.0, 0.0)
            for j in range(3):
                rows = jnp.where(sub_aug == 3 * group + j, splits_t[j][h:h + 1, :], rows)
            qat_ref[h, HEAD_DIM:HEAD_DIM + AUG, :] = rows.astype(BF16)
            qat_ref[h, HEAD_DIM + AUG:, :] = jnp.zeros((HEAD_DIM - AUG, tm), BF16)
    del n_heads
    st_ref[0, 0:1, :] = qstat
    st_ref[0, 1:2, :] = kstat
    st_ref[0, 2:3, :] = c2[tm - 1:tm, :]
    st_ref[0, 3:4, :] = c2[0:1, :]
    st_ref[0, 4:, :] = jnp.zeros((SUBLANES - 4, LANES), F32)


def _prep2(u, lf, c0, *, tm, col_q, d_attn, n_kv, group):
    m = u.shape[0]
    nt = m // tm
    n_heads = n_kv * group
    dkv = n_kv * HEAD_DIM
    assert 3 * group + 3 <= AUG
    cq_blk, ck_blk, cv_blk = col_q // d_attn, (col_q + d_attn) // dkv, (col_q + d_attn + dkv) // dkv
    kern = functools.partial(_prep2_kernel, tm=tm, n_kv=n_kv, group=group)
    return pl.pallas_call(
        kern,
        grid=(nt,),
        in_specs=[
            pl.BlockSpec((tm, d_attn), lambda i: (i, cq_blk)),
            pl.BlockSpec((tm, dkv), lambda i: (i, ck_blk)),
            pl.BlockSpec((tm, dkv), lambda i: (i, cv_blk)),
            pl.BlockSpec((tm, LANES), lambda i: (i, 0)),
            pl.BlockSpec((1, LANES), lambda i: (0, 0)),
        ],
        out_specs=[
            pl.BlockSpec((n_heads, 2 * HEAD_DIM, tm), lambda i: (0, 0, i)),
            pl.BlockSpec((n_kv, 1, tm, 2 * HEAD_DIM), lambda i: (0, i, 0, 0)),
            pl.BlockSpec((n_kv, 1, HEAD_DIM, tm), lambda i: (0, i, 0, 0)),
            pl.BlockSpec((1, SUBLANES, LANES), lambda i: (i, 0, 0)),
        ],
        out_shape=[
            jax.ShapeDtypeStruct((n_heads, 2 * HEAD_DIM, m), BF16),
            jax.ShapeDtypeStruct((n_kv, nt, tm, 2 * HEAD_DIM), BF16),
            jax.ShapeDtypeStruct((n_kv, nt, HEAD_DIM, tm), BF16),
            jax.ShapeDtypeStruct((nt, SUBLANES, LANES), F32),
        ],
        scratch_shapes=[pltpu.VMEM((1, LANES), F32)],
        compiler_params=_cparams(("arbitrary",)),
        name="attn_prep",
    )(u, u, u, lf, c0)


def _attn2_kernel(nt_ref, pf_ref, qat_ref, ka_ref, vt_ref, kam_ref, vtm_ref, o_ref, sa_buf, sb_buf, *, t, nq):
    h = pl.program_id(0)
    qi = pl.program_id(1)
    q = qat_ref[...]

    def scores(ka):
        return jnp.dot(ka, q, preferred_element_type=F32)

    def absorb(s, smax, vt, carry):
        m_prev, l_prev, acc = carry
        m_new = jnp.maximum(m_prev, smax)
        alpha = jnp.exp2(m_prev - m_new)
        p = jnp.exp2(s - m_new)
        l_new = alpha * l_prev + jnp.sum(p, axis=0, keepdims=True)
        acc = alpha * acc + jnp.dot(vt, p.astype(BF16), preferred_element_type=F32)
        return m_new, l_new, acc

    carry = (jnp.full((1, t), NEG_BIG, F32), jnp.zeros((1, t), F32), jnp.zeros((HEAD_DIM, t), F32))
    key = lax.broadcasted_iota(jnp.int32, (t, t), 0)
    qry = lax.broadcasted_iota(jnp.int32, (t, t), 1)
    s_diag = jnp.where(key <= qry, scores(ka_ref[qi]), NEG_BIG)
    sa_buf[...] = s_diag
    n_older = nt_ref[h * nq + qi]
    colmax = lambda s: jnp.max(s, axis=0, keepdims=True)

    def step(k_next, src, dst, smax_src, carry):
        s_next = scores(ka_ref[k_next])
        dst[...] = s_next
        return colmax(s_next), absorb(src[...], smax_src, vt_ref[k_next + 1], carry)

    def older_pair(j, state):
        smax_a, carry = state
        k1 = qi - 1 - 2 * j
        smax_b, carry = step(k1, sa_buf, sb_buf, smax_a, carry)
        return step(k1 - 1, sb_buf, sa_buf, smax_b, carry)

    smax_a, carry = lax.fori_loop(0, n_older // 2, older_pair, (colmax(s_diag), carry))
    k_last = qi - n_older

    def odd_tail(carry):
        smax_b, carry = step(k_last, sa_buf, sb_buf, smax_a, carry)
        return absorb(sb_buf[...], smax_b, vt_ref[k_last], carry)

    carry = lax.cond(n_older % 2 == 1, odd_tail, lambda c: absorb(sa_buf[...], smax_a, vt_ref[k_last], c), carry)
    def prefix(carry):
        s_pre = scores(kam_ref[...])
        return absorb(s_pre, colmax(s_pre), vtm_ref[...], carry)

    _, l_fin, acc = lax.cond(pf_ref[h * nq + qi] == 1, prefix, lambda c: c, carry)
    o_ref[...] = (acc / l_fin).T


def _attn2(ntab, ptab, qat, ka, vt, kam, vtm, *, t, group):
    n_heads, _, m = qat.shape
    n_kv, nk = ka.shape[:2]
    nq = m // t
    npad = kam.shape[1]
    kern = functools.partial(_attn2_kernel, t=t, nq=nq)
    grid_spec = pltpu.PrefetchScalarGridSpec(
        num_scalar_prefetch=2,
        grid=(n_heads, nq),
        in_specs=[
            pl.BlockSpec((None, 2 * HEAD_DIM, t), lambda h, qi, nt, pf: (h, 0, qi)),
            pl.BlockSpec((None, nk, t, 2 * HEAD_DIM), lambda h, qi, nt, pf: (h // group, 0, 0, 0)),
            pl.BlockSpec((None, nk, HEAD_DIM, t), lambda h, qi, nt, pf: (h // group, 0, 0, 0)),
            pl.BlockSpec((None, npad, 2 * HEAD_DIM), lambda h, qi, nt, pf: (h // group, 0, 0)),
            pl.BlockSpec((None, HEAD_DIM, npad), lambda h, qi, nt, pf: (h // group, 0, 0)),
        ],
        out_specs=pl.BlockSpec((t, HEAD_DIM), lambda h, qi, nt, pf: (qi, h)),
        scratch_shapes=[pltpu.VMEM((t, t), F32), pltpu.VMEM((t, t), F32)],
    )
    return pl.pallas_call(
        kern,
        grid_spec=grid_spec,
        out_shape=jax.ShapeDtypeStruct((m, n_heads * HEAD_DIM), F32),
        compiler_params=_cparams(("arbitrary", "arbitrary")),
        name="fox_attn",
    )(ntab, ptab, qat, ka, vt, kam, vtm)


def _tile_schedule(stats, k2_prefix_max, c2_prefix_last, *, n_kv, group):
    n_heads = n_kv * group
    nt = stats.shape[0]
    qmax = jnp.sqrt(stats[:, 0, :n_heads])
    kmax = jnp.sqrt(jnp.maximum(jnp.max(stats[:, 1, :n_kv], axis=0), k2_prefix_max))
    smax = qmax * jnp.repeat(kmax, group)[None, :]
    c_last = stats[:, 2, :n_heads]
    c_first = stats[:, 3, :n_heads]
    bias_max = c_first[:, None, :] - c_last[None, :, :]
    needed = 2.0 * smax[:, None, :] + bias_max > -SKIP_LOG2
    qi = jnp.arange(nt)[:, None, None]
    kt = jnp.arange(nt)[None, :, None]
    dist = jnp.where(jnp.logical_and(needed, kt < qi), qi - kt, 0)
    n_older = jnp.max(dist, axis=1)
    prefix_needed = 2.0 * smax + (c_first - c2_prefix_last[None, :]) > -SKIP_LOG2
    flat = lambda a: a.T.astype(jnp.int32).reshape(-1)
    return flat(n_older), flat(prefix_needed)


def _attn_small_kernel(q_ref, k_ref, v_ref, lf_ref, ck_ref, cv_ref, clf_ref, o_ref, c_ref, ccum_s,
                       *, n_kv, group, n_cached_seqs, p_len, s_len):
    b = pl.program_id(0)
    has_cache = b < n_cached_seqs
    n_blk = p_len // LANES
    tri = _lower_tri(LANES)

    def cblock(j, off):
        r0 = pl.multiple_of(j * LANES, LANES)
        cb = _split3_dot(tri, clf_ref[pl.ds(r0, LANES), :]) + off
        ccum_s[pl.ds(r0, LANES), :] = cb
        return cb[LANES - 1:LANES, :]

    total = lax.fori_loop(0, n_blk, cblock, jnp.zeros((1, LANES), F32))
    total = jnp.where(has_cache, total, 0.0)
    zrows = jnp.zeros((LANES - s_len, LANES), F32)
    c_pad = _split3_dot(tri, jnp.concatenate([lf_ref[...], zrows], axis=0)) + total
    c_own = c_pad[:s_len, :]
    c_ref[...] = c_own
    c_own_t = c_pad.T
    c_cache_t = ccum_s[...].T

    rows = lax.broadcasted_iota(jnp.int32, (s_len, LANES), 0)
    cols = lax.broadcasted_iota(jnp.int32, (s_len, LANES), 1)
    scale = HEAD_DIM ** -0.5
    for g in range(n_kv):
        ksl = slice(g * HEAD_DIM, (g + 1) * HEAD_DIM)
        k_own = jnp.concatenate([k_ref[:, ksl], zrows], axis=0).astype(BF16)
        v_own = jnp.concatenate([v_ref[:, ksl], zrows], axis=0).astype(BF16)
        k_cache = ck_ref[:, ksl].astype(BF16)
        v_cache = cv_ref[:, ksl].astype(BF16)
        for r in range(group):
            h = g * group + r
            q = (q_ref[:, h * HEAD_DIM:(h + 1) * HEAD_DIM] * scale).astype(BF16)
            cqh = c_own[:, h:h + 1]
            s_own = _qk(q, k_own) + (cqh - c_own_t[h:h + 1, :])
            s_own = jnp.where(cols <= rows, s_own, NEG_BIG)
            s_cache = _qk(q, k_cache) + (cqh - c_cache_t[h:h + 1, :])
            s_cache = jnp.where(has_cache, s_cache, NEG_BIG)
            mx = jnp.maximum(jnp.max(s_own, axis=1, keepdims=True), jnp.max(s_cache, axis=1, keepdims=True))
            p_own = jnp.exp(s_own - mx)
            p_cache = jnp.exp(s_cache - mx)
            den = jnp.sum(p_own, axis=1, keepdims=True) + jnp.sum(p_cache, axis=1, keepdims=True)
            num = jnp.dot(p_own.astype(BF16), v_own, preferred_element_type=F32)
            num = num + jnp.dot(p_cache.astype(BF16), v_cache, preferred_element_type=F32)
            o_ref[:, h * HEAD_DIM:(h + 1) * HEAD_DIM] = num / den


def _attn_small(u, lf, cache_k, cache_v, cache_lf, *, n_seq, s_len, col_q, d_attn, n_kv, group):
    n_cached, p_len, dkv = cache_k.shape
    ck_blk, cv_blk = (col_q + d_attn) // dkv, (col_q + d_attn + dkv) // dkv
    cq_blk = col_q // d_attn
    cmap = lambda b: (jnp.minimum(b, n_cached - 1), 0, 0)
    kern = functools.partial(_attn_small_kernel, n_kv=n_kv, group=group, n_cached_seqs=n_cached, p_len=p_len,
                             s_len=s_len)
    return pl.pallas_call(
        kern,
        grid=(n_seq,),
        in_specs=[
            pl.BlockSpec((s_len, d_attn), lambda b: (b, cq_blk)),
            pl.BlockSpec((s_len, dkv), lambda b: (b, ck_blk)),
            pl.BlockSpec((s_len, dkv), lambda b: (b, cv_blk)),
            pl.BlockSpec((s_len, LANES), lambda b: (b, 0)),
            pl.BlockSpec((None, p_len, dkv), cmap),
            pl.BlockSpec((None, p_len, dkv), cmap),
            pl.BlockSpec((None, p_len, LANES), cmap),
        ],
        out_specs=[pl.BlockSpec((s_len, d_attn), lambda b: (b, 0)),
                   pl.BlockSpec((s_len, LANES), lambda b: (b, 0))],
        out_shape=[jax.ShapeDtypeStruct((n_seq * s_len, d_attn), F32),
                   jax.ShapeDtypeStruct((n_seq * s_len, LANES), F32)],
        scratch_shapes=[pltpu.VMEM((p_len, LANES), F32)],
        compiler_params=_cparams(("arbitrary",)),
        name="fox_attn_small",
    )(u, u, u, lf, cache_k, cache_v, cache_lf)


def _outproj_kernel(orn_ref, oat_ref, ga_ref, w_ref, x_ref, h_ref, a_s, *, d_rnn):
    @pl.when(pl.program_id(1) == 0)
    def _():
        a_s[:, :d_rnn] = orn_ref[...]
        a_s[:, d_rnn:] = _rms(oat_ref[...], ga_ref[...]).astype(BF16)

    h_ref[...] = x_ref[...] + jnp.dot(a_s[...], w_ref[...], preferred_element_type=F32)


def _outproj(orn, oat, ga, w, x, *, tm, tn):
    m, d_rnn = orn.shape
    d_attn = oat.shape[1]
    d = x.shape[1]
    return pl.pallas_call(
        functools.partial(_outproj_kernel, d_rnn=d_rnn),
        grid=(m // tm, d // tn),
        in_specs=[
            pl.BlockSpec((tm, d_rnn), lambda i, j: (i, 0)),
            pl.BlockSpec((tm, d_attn), lambda i, j: (i, 0)),
            pl.BlockSpec((1, d_attn), lambda i, j: (0, 0)),
            pl.BlockSpec((d_rnn + d_attn, tn), lambda i, j: (0, j)),
            pl.BlockSpec((tm, tn), lambda i, j: (i, j)),
        ],
        out_specs=pl.BlockSpec((tm, tn), lambda i, j: (i, j)),
        out_shape=jax.ShapeDtypeStruct((m, d), F32),
        scratch_shapes=[pltpu.VMEM((tm, d_rnn + d_attn), BF16)],
        compiler_params=_cparams(("arbitrary", "arbitrary")),
        name="outproj",
    )(orn, oat, ga, w, x)


def _norm_cast_kernel(h_ref, g_ref, o_ref):
    o_ref[...] = _rms(h_ref[...], g_ref[...]).astype(BF16)


def _norm_cast(h, g, *, tm):
    m, d = h.shape
    return pl.pallas_call(
        _norm_cast_kernel,
        grid=(m // tm,),
        in_specs=[pl.BlockSpec((tm, d), lambda i: (i, 0)), pl.BlockSpec((1, d), lambda i: (0, 0))],
        out_specs=pl.BlockSpec((tm, d), lambda i: (i, 0)),
        out_shape=jax.ShapeDtypeStruct((m, d), BF16),
        compiler_params=_cparams(("arbitrary",)),
        name="norm_cast",
    )(h, g)


def _ffn_in_kernel(xn_ref, wg_ref, wv_ref, wc_ref, bc_ref, first_ref, p1_ref, p2_ref, act_ref, zt_ref,
                   zp_s, carry_s, *, tm, seq_len, multi_seq, n_sub):
    i = pl.program_id(0)
    f = pl.program_id(1)

    if not multi_seq:
        @pl.when(i > 0)
        def _():
            zp_s[0:SUBLANES, :] = carry_s[f]

    rs = tm // n_sub
    for r0 in range(0, tm, rs):
        xn = xn_ref[r0:r0 + rs, :]
        zg = jnp.dot(xn, wg_ref[...], preferred_element_type=F32)
        zv = jnp.dot(xn, wv_ref[...], preferred_element_type=F32)
        prev1, prev2 = _prev_rows(zp_s, zg, first_ref, (p1_ref, p2_ref), rows=rs, seq_len=seq_len,
                                  multi_seq=multi_seq, r0=r0)
        gate = bc_ref[...] + wc_ref[0:1, :] * prev2
        gate = gate + wc_ref[1:2, :] * prev1
        gate = gate + wc_ref[2:3, :] * zg
        if multi_seq:
            zt_ref[r0:r0 + rs, :] = zg
        elif r0 + rs == tm:
            tail = zg[rs - SUBLANES:, :]
            carry_s[f] = tail
            zt_ref[...] = tail
        act_ref[r0:r0 + rs, :] = (_gelu(gate) * zv).astype(BF16)


def _ffn_in(xn, w, wc, bc, ffn_state, *, tm, tf, seq_len):
    m, d = xn.shape
    d_ff = w.shape[1] // 2
    nf = d_ff // tf
    multi_seq = m // seq_len > 1
    first, states = _conv_state_inputs(ffn_state, rows=tm, seq_len=seq_len, multi_seq=multi_seq)
    zt_rows = tm if multi_seq else SUBLANES
    st_spec = lambda a: pl.BlockSpec((a.shape[0], tf), lambda i, f: (0, f))
    sub_rows = 2 * LANES
    return pl.pallas_call(
        functools.partial(_ffn_in_kernel, tm=tm, seq_len=seq_len, multi_seq=multi_seq,
                          n_sub=tm // sub_rows if tm % sub_rows == 0 else 1),
        grid=(m // tm, nf),
        in_specs=[
            pl.BlockSpec((tm, d), lambda i, f: (i, 0)),
            pl.BlockSpec((d, tf), lambda i, f: (0, f)),
            pl.BlockSpec((d, tf), lambda i, f: (0, f + nf)),
            pl.BlockSpec((wc.shape[0], tf), lambda i, f: (0, f)),
            pl.BlockSpec((1, tf), lambda i, f: (0, f)),
            st_spec(first)] + [st_spec(s) for s in states],
        out_specs=[pl.BlockSpec((tm, tf), lambda i, f: (i, f)),
                   pl.BlockSpec((zt_rows, tf), lambda i, f: (i, f))],
        out_shape=[jax.ShapeDtypeStruct((m, d_ff), BF16),
                   jax.ShapeDtypeStruct((m // tm * zt_rows, d_ff), F32)],
        scratch_shapes=[pltpu.VMEM((tm + SUBLANES, tf), F32), pltpu.VMEM((nf, SUBLANES, tf), F32)],
        compiler_params=_cparams(("arbitrary", "arbitrary")),
        name="ffn_in",
    )(xn, w, w, wc, bc, first, *states)


def _ffn_out_kernel(act_ref, w_ref, h_ref, g_ref, y_ref, *, tc):
    k = pl.program_id(1)
    d = y_ref.shape[1]
    chunks = [slice(c, c + tc) for c in range(0, d, tc)]

    @pl.when(k == 0)
    def _():
        y_ref[...] = h_ref[...]

    act = act_ref[...]
    for sl in chunks:
        y_ref[:, sl] += jnp.dot(act, w_ref[:, sl], preferred_element_type=F32)

    @pl.when(k == pl.num_programs(1) - 1)
    def _():
        ssq = jnp.zeros((y_ref.shape[0], 1), F32)
        for sl in chunks:
            yc = y_ref[:, sl]
            ssq = ssq + jnp.sum(yc * yc, axis=-1, keepdims=True)
        inv = lax.rsqrt(ssq / d + EPS)
        for sl in chunks:
            y_ref[:, sl] = y_ref[:, sl] * inv * g_ref[:, sl]


def _ffn_out(act, w, h, g, *, tm, tk):
    m, d_ff = act.shape
    d = h.shape[1]
    return pl.pallas_call(
        functools.partial(_ffn_out_kernel, tc=_pick(d, 1024)),
        grid=(m // tm, d_ff // tk),
        in_specs=[
            pl.BlockSpec((tm, tk), lambda i, k: (i, k)),
            pl.BlockSpec((tk, d), lambda i, k: (k, 0)),
            pl.BlockSpec((tm, d), lambda i, k: (i, 0)),
            pl.BlockSpec((1, d), lambda i, k: (0, 0)),
        ],
        out_specs=pl.BlockSpec((tm, d), lambda i, k: (i, 0)),
        out_shape=jax.ShapeDtypeStruct((m, d), F32),
        compiler_params=_cparams(("arbitrary", "arbitrary")),
        name="ffn_out",
    )(act, w, h, g)


def _pick(n, pref):
    t = min(n, pref)
    while n % t:
        t //= 2
    return t


def _place_state(state, seq_len, back):
    n_seq, w, c = state.shape
    out = jnp.zeros((n_seq, seq_len, c), state.dtype)
    out = out.at[:, :back, :].set(state[:, w - back:, :])
    return out.reshape(n_seq * seq_len, c)


def kernel(x_prompt, x_sample, cache_k, cache_v, cache_logf, state_rnn_h, state_rnn_conv, state_ffn_conv,
           meta_tokens, g_mix, w_in, b_f, w_rnn_conv, b_rnn_conv, w_rg_a, b_rg_a, w_rg_x, b_rg_x, lru_lambda,
           g_out_rnn, g_out_attn, w_out, g_ffn, w_ffn_in, w_ffn_conv, b_ffn_conv, w_ffn_out, g_final):
    depth = g_mix.shape[0]
    assert depth == 1, "single-layer stack"
    batch, seq, d_model = x_prompt.shape
    assert batch == 1
    dec_batch, dec_seq, _ = x_sample.shape
    n_meta = meta_tokens.shape[0]
    assert n_meta == dec_seq, "meta prefix is run as one more sample-length sequence"
    n_heads = b_f.shape[1]
    n_kv = cache_k.shape[3]
    group = n_heads // n_kv
    assert group <= SUBLANES and cache_k.shape[4] == HEAD_DIM
    d_attn = n_heads * HEAD_DIM
    d_rnn = state_rnn_h.shape[2]
    dkv = n_kv * HEAD_DIM
    d_ff = state_ffn_conv.shape[3]
    past = cache_k.shape[2]
    col_q = 2 * d_rnn
    n_main_cols = col_q + d_attn + 2 * dkv
    n_rnn_blocks = w_rg_a.shape[1]
    rnn_w = w_rnn_conv.shape[1]
    ffn_w = w_ffn_conv.shape[1]
    assert rnn_w == 4 and ffn_w == 3

    w_in_b = w_in[0][:, :n_main_cols].astype(BF16)
    wf_b = jnp.pad(w_in[0][:, n_main_cols:], ((0, 0), (0, LANES - n_heads))).astype(BF16)
    bf_p = jnp.pad(b_f[0], (0, LANES - n_heads)).reshape(1, LANES)
    w_out_b = w_out[0].astype(BF16)
    w_ffn_in_b = w_ffn_in[0].astype(BF16)
    w_ffn_out_b = w_ffn_out[0].astype(BF16)
    wg = jnp.concatenate([w_rg_a[0], w_rg_x[0]], axis=-1).astype(BF16)
    row = lambda a: a.reshape(1, -1)
    rg_args = (w_rnn_conv[0], row(b_rnn_conv[0]), wg, row(b_rg_a[0]), row(b_rg_x[0]), row(lru_lambda[0]),
               row(g_out_rnn[0]))

    def chain(x, seq_len, rnn_state, h0, ffn_state, attn_fn, tm, tm_ffn, tr, tn, tf, tk):
        u, lf = _inproj(x, row(g_mix[0]), w_in_b, wf_b, bf_p, tm=tm, tn=tn)
        orn, h_last = _rglru(u, rnn_state, h0, *rg_args, tr=tr, seq_len=seq_len, d_rnn=d_rnn)
        oat, attn_aux = attn_fn(u, lf)
        h1 = _outproj(orn, oat, row(g_out_attn[0]), w_out_b, x, tm=tm, tn=_pick(d_model, tn))
        xn = _norm_cast(h1, row(g_ffn[0]), tm=tm)
        act, zt = _ffn_in(xn, w_ffn_in_b, w_ffn_conv[0], row(b_ffn_conv[0]), ffn_state,
                          tm=tm_ffn, tf=tf, seq_len=seq_len)
        y = _ffn_out(act, w_ffn_out_b, h1, row(g_final), tm=tm, tk=tk)
        return u, lf, h_last, zt, y, attn_aux

    n_small = dec_batch + 1
    ms = n_small * dec_seq
    xs = jnp.concatenate([x_sample.reshape(dec_batch * dec_seq, d_model), meta_tokens.astype(F32)], axis=0)
    zpad = lambda a: jnp.concatenate([a, jnp.zeros((1,) + a.shape[1:], a.dtype)], axis=0)
    clf = jnp.pad(cache_logf[0], ((0, 0), (0, 0), (0, LANES - n_heads)))

    def small_attn(u, lf):
        return _attn_small(u, lf, cache_k[0].reshape(dec_batch, past, dkv), cache_v[0].reshape(dec_batch, past, dkv),
                           clf, n_seq=n_small, s_len=dec_seq, col_q=col_q, d_attn=d_attn, n_kv=n_kv, group=group)

    tn, tf, tk = _pick(n_main_cols, 1024), _pick(d_ff, 512), _pick(d_ff, 1024)
    u_s, lf_s, hl_s, zt_s, y_s, c_s = chain(
        xs, dec_seq, zpad(state_rnn_conv[0]), zpad(state_rnn_h[0]), zpad(state_ffn_conv[0]), small_attn,
        tm=ms, tm_ffn=ms, tr=ms, tn=tn, tf=tf, tk=tk)

    m0 = dec_batch * dec_seq
    u_meta = u_s[m0:]
    k_meta = u_meta[:, col_q + d_attn:col_q + d_attn + dkv]
    v_meta = u_meta[:, col_q + d_attn + dkv:]
    c_meta = c_s[m0:]
    npad = -(-n_meta // 16) * 16
    c2m = (c_meta[:, :n_heads] * LOG2E).reshape(n_meta, n_kv, group).transpose(1, 0, 2)
    c2m = jnp.pad(c2m, ((0, 0), (0, npad - n_meta), (0, 0)), constant_values=-NEG_BIG)
    bias_cols = jnp.stack(_bf16_split3(c2m), axis=-1).reshape(n_kv, npad, 3 * group)
    bias_cols = jnp.concatenate([bias_cols, jnp.ones((n_kv, npad, 3), BF16),
                                 jnp.zeros((n_kv, npad, HEAD_DIM - 3 * group - 3), BF16)], axis=-1)
    k_meta_p = jnp.pad(k_meta, ((0, npad - n_meta), (0, 0))).reshape(npad, n_kv, HEAD_DIM).transpose(1, 0, 2)
    kam = jnp.concatenate([k_meta_p.astype(BF16), bias_cols], axis=-1)
    vtm = jnp.pad(v_meta, ((0, npad - n_meta), (0, 0))).reshape(npad, n_kv, HEAD_DIM).transpose(1, 2, 0)
    vtm = vtm.astype(BF16)
    tq = _pick(seq, 512)

    def main_attn(u, lf):
        qat, ka, vt, stats = _prep2(u, lf, c_meta[n_meta - 1:], tm=tq, col_q=col_q, d_attn=d_attn, n_kv=n_kv,
                                    group=group)
        k_meta_f = k_meta.astype(BF16).astype(F32).reshape(n_meta, n_kv, HEAD_DIM)
        ntab, ptab = _tile_schedule(stats, jnp.max(jnp.sum(k_meta_f * k_meta_f, axis=-1), axis=0),
                                    c_meta[n_meta - 1, :n_heads] * LOG2E, n_kv=n_kv, group=group)
        return _attn2(ntab, ptab, qat, ka, vt, kam, vtm, t=tq, group=group), None

    u_m, lf_m, hl_m, zt_m, y_m, _ = chain(
        x_prompt[0], seq, u_meta[None, n_meta - (rnn_w - 1):, :d_rnn], hl_s[dec_batch:],
        zt_s[None, ms - (ffn_w - 1):, :], main_attn,
        tm=_pick(seq, 512), tm_ffn=_pick(seq, 1024), tr=_pick(seq, 256), tn=tn, tf=tf, tk=tk)

    kcols = slice(col_q + d_attn, col_q + d_attn + dkv)
    vcols = slice(col_q + d_attn + dkv, n_main_cols)
    y_prompt = y_m[None]
    y_sample = y_s[:m0].reshape(dec_batch, dec_seq, d_model)
    new_k_prompt = jnp.concatenate([k_meta, u_m[:, kcols]], axis=0).reshape(1, 1, n_meta + seq, n_kv, HEAD_DIM)
    new_v_prompt = jnp.concatenate([v_meta, u_m[:, vcols]], axis=0).reshape(1, 1, n_meta + seq, n_kv, HEAD_DIM)
    new_logf_prompt = jnp.concatenate([lf_s[m0:, :n_heads], lf_m[:, :n_heads]], axis=0)[None, None]
    new_rnn_h_prompt = hl_m[None]
    new_rnn_conv_prompt = u_m[seq - (rnn_w - 1):, :d_rnn][None, None]
    new_ffn_conv_prompt = zt_m[zt_m.shape[0] - (ffn_w - 1):][None, None]
    us3 = u_s[:m0].reshape(dec_batch, dec_seq, n_main_cols)
    new_k_sample = us3[:, :, kcols].reshape(1, dec_batch, dec_seq, n_kv, HEAD_DIM)
    new_v_sample = us3[:, :, vcols].reshape(1, dec_batch, dec_seq, n_kv, HEAD_DIM)
    new_logf_sample = lf_s[:m0, :n_heads].reshape(1, dec_batch, dec_seq, n_heads)
    new_rnn_h_sample = hl_s[:dec_batch][None]
    new_rnn_conv_sample = us3[:, dec_seq - (rnn_w - 1):, :d_rnn][None]
    new_ffn_conv_sample = zt_s[:m0].reshape(dec_batch, dec_seq, d_ff)[:, dec_seq - (ffn_w - 1):][None]
    return (y_prompt, y_sample, new_k_prompt, new_v_prompt, new_logf_prompt, new_rnn_h_prompt,
            new_rnn_conv_prompt, new_ffn_conv_prompt, new_k_sample, new_v_sample, new_logf_sample,
            new_rnn_h_sample, new_rnn_conv_sample, new_ffn_conv_sample)
```

```python
import functools

import jax
import jax.numpy as jnp
from jax import lax
from jax.experimental import pallas as pl
from jax.experimental.pallas import tpu as pltpu

F32 = jnp.float32
BF16 = jnp.bfloat16

EPS = 1e-6
LRU_C = 8.0
HEAD_DIM = 128
LANES = 128
SUBLANES = 8
NEG_BIG = -1e30
VMEM_LIMIT = 56 * 1024 * 1024


def _cparams(sem):
    return pltpu.CompilerParams(dimension_semantics=sem, vmem_limit_bytes=VMEM_LIMIT)


def _rms(x, g):
    return x * lax.rsqrt(jnp.mean(x * x, axis=-1, keepdims=True) + EPS) * g


def _softplus(x):
    return jnp.maximum(x, 0.0) + jnp.log1p(jnp.exp(-jnp.abs(x)))


def _gelu(x):
    return jax.nn.gelu(x, approximate=True)


def _sigmoid(x):
    return 0.5 * jnp.tanh(0.5 * x) + 0.5


def _split3_dot(l_bf16, x):
    hi = x.astype(BF16)
    r1 = x - hi.astype(F32)
    mid = r1.astype(BF16)
    lo = (r1 - mid.astype(F32)).astype(BF16)
    out = jnp.dot(l_bf16, hi, preferred_element_type=F32)
    out = out + jnp.dot(l_bf16, mid, preferred_element_type=F32)
    return out + jnp.dot(l_bf16, lo, preferred_element_type=F32)


def _lower_tri(n):
    r = lax.broadcasted_iota(jnp.int32, (n, n), 0)
    c = lax.broadcasted_iota(jnp.int32, (n, n), 1)
    return jnp.where(c <= r, 1.0, 0.0).astype(BF16)


def _inproj_kernel(x_ref, g_ref, w_ref, wf_ref, bf_ref, u_ref, lf_ref, xn_ref):
    @pl.when(pl.program_id(1) == 0)
    def _():
        xb = _rms(x_ref[...], g_ref[...]).astype(BF16)
        xn_ref[...] = xb
        f = jnp.dot(xb, wf_ref[...], preferred_element_type=F32) + bf_ref[...]
        lf_ref[...] = -_softplus(-f)

    u_ref[...] = jnp.dot(xn_ref[...], w_ref[...], preferred_element_type=F32)


def _inproj(x, g, w, wf, bf, *, tm, tn):
    m, d = x.shape
    nc = w.shape[1]
    return pl.pallas_call(
        _inproj_kernel,
        grid=(m // tm, nc // tn),
        in_specs=[
            pl.BlockSpec((tm, d), lambda i, j: (i, 0)),
            pl.BlockSpec((1, d), lambda i, j: (0, 0)),
            pl.BlockSpec((d, tn), lambda i, j: (0, j)),
            pl.BlockSpec((d, LANES), lambda i, j: (0, 0)),
            pl.BlockSpec((1, LANES), lambda i, j: (0, 0)),
        ],
        out_specs=[
            pl.BlockSpec((tm, tn), lambda i, j: (i, j)),
            pl.BlockSpec((tm, LANES), lambda i, j: (i, 0)),
        ],
        out_shape=[jax.ShapeDtypeStruct((m, nc), F32), jax.ShapeDtypeStruct((m, LANES), F32)],
        scratch_shapes=[pltpu.VMEM((tm, d), BF16)],
        compiler_params=_cparams(("arbitrary", "arbitrary")),
        name="inproj",
    )(x, g, w, wf, bf)


def _prev_rows(x, before, states, *, seq_len, multi_seq):
    rows, c = x.shape
    prevs = []
    for k in range(1, len(states) + 1):
        rolled = pltpu.roll(x, k, axis=0)
        if multi_seq:
            pos = lax.broadcasted_iota(jnp.int32, (rows, 1), 0) % seq_len
            pk = jnp.where(pos >= k, rolled, states[k - 1][...])
        else:
            sub = lax.broadcasted_iota(jnp.int32, (SUBLANES, c), 0)
            head = jnp.where(sub < k, pltpu.roll(before, k, axis=0), rolled[:SUBLANES, :])
            pk = jnp.concatenate([head, rolled[SUBLANES:, :]], axis=0)
        prevs.append(pk)
    return prevs


def _rglru_kernel(xr_ref, yg_ref, first_ref, p1_ref, p2_ref, p3_ref, h0_ref, wc_ref, bc_ref, wg_ref, ba_ref,
                  bx_ref, lam_ref, go_ref, o_ref, hl_ref, xp_s, a_s, b_s, h_s, *, tr, seq_len, n_blocks,
                  multi_seq):
    i = pl.program_id(0)
    d = xr_ref.shape[1]
    blk = d // n_blocks
    x = xr_ref[...]

    @pl.when(i == 0)
    def _():
        h_s[...] = jnp.zeros_like(h_s)
        xp_s[...] = first_ref[...]

    prev1, prev2, prev3 = _prev_rows(x, xp_s[...], (p1_ref, p2_ref, p3_ref), seq_len=seq_len,
                                     multi_seq=multi_seq)
    xc = bc_ref[...] + wc_ref[0:1, :] * prev3
    xc = xc + wc_ref[1:2, :] * prev2
    xc = xc + wc_ref[2:3, :] * prev1
    xc = xc + wc_ref[3:4, :] * x
    xp_s[...] = x[tr - SUBLANES:, :]

    sp = _softplus(-lam_ref[...])
    xcb = xc.astype(BF16)
    for n in range(n_blocks):
        sl = slice(n * blk, (n + 1) * blk)
        gn = jnp.dot(xcb[:, sl], wg_ref[n], preferred_element_type=F32)
        r = _sigmoid(gn[:, :blk] + ba_ref[:, sl])
        ig = _sigmoid(gn[:, blk:] + bx_ref[:, sl])
        log_a = (-LRU_C) * r * sp[:, sl]
        a = jnp.exp(log_a)
        a_s[:, sl] = a
        y = -jnp.tanh(log_a) * (a * a + 1.0)
        root = jnp.where(y > 0.0, y * lax.rsqrt(y), 0.0)
        b_s[:, sl] = root * (ig * xc[:, sl])

    row = lax.broadcasted_iota(jnp.int32, (SUBLANES, d), 0)
    groups_per_seq = seq_len // SUBLANES

    def group(gi, h):
        r0 = pl.multiple_of(gi * SUBLANES, SUBLANES)
        gg = i * (tr // SUBLANES) + gi
        seq = gg // groups_per_seq
        first = (gg % groups_per_seq) == 0
        h = jnp.where(first, h0_ref[pl.ds(seq, 1), :], h)
        a8 = a_s[pl.ds(r0, SUBLANES), :]
        b8 = b_s[pl.ds(r0, SUBLANES), :]
        for sh in (1, 2, 4):
            ok = row >= sh
            b8 = jnp.where(ok, a8 * pltpu.roll(b8, sh, axis=0) + b8, b8)
            a8 = jnp.where(ok, a8 * pltpu.roll(a8, sh, axis=0), a8)
        h8 = a8 * h + b8
        b_s[pl.ds(r0, SUBLANES), :] = h8
        hn = h8[SUBLANES - 1:SUBLANES, :]

        @pl.when((gg % groups_per_seq) == groups_per_seq - 1)
        def _():
            hl_ref[pl.ds(seq, 1), :] = hn
        return hn

    h_s[...] = lax.fori_loop(0, tr // SUBLANES, group, h_s[...])

    o = b_s[...] * _gelu(yg_ref[...])
    o_ref[...] = _rms(o, go_ref[...]).astype(BF16)


def _conv_state_inputs(state, *, rows, seq_len, multi_seq):
    n_seq, w, c = state.shape
    dummy = jnp.zeros((SUBLANES, c), F32)
    if multi_seq:
        assert rows == n_seq * seq_len
        first = dummy
        states = [_place_state(state, seq_len, k) for k in range(1, w + 1)]
    else:
        assert n_seq == 1
        first = jnp.pad(state[0], ((SUBLANES - w, 0), (0, 0)))
        states = [dummy] * w
    return first, states


def _rglru(u, rnn_state, h0, wc, bc, wg, ba, bx, lam, go, *, tr, seq_len, d_rnn):
    m = u.shape[0]
    n_seq = m // seq_len
    multi_seq = n_seq > 1
    n_blocks = wg.shape[0]
    first, states = _conv_state_inputs(rnn_state, rows=tr, seq_len=seq_len, multi_seq=multi_seq)
    row_spec = lambda c: pl.BlockSpec((tr, d_rnn), lambda i: (i, c))
    full = lambda a: pl.BlockSpec(a.shape, lambda i: (0,) * a.ndim)
    kern = functools.partial(_rglru_kernel, tr=tr, seq_len=seq_len, n_blocks=n_blocks, multi_seq=multi_seq)
    return pl.pallas_call(
        kern,
        grid=(m // tr,),
        in_specs=[row_spec(0), row_spec(1), full(first)] + [full(s) for s in states] + [
            full(h0), full(wc), full(bc), full(wg), full(ba), full(bx), full(lam), full(go)],
        out_specs=[pl.BlockSpec((tr, d_rnn), lambda i: (i, 0)), pl.BlockSpec((n_seq, d_rnn), lambda i: (0, 0))],
        out_shape=[jax.ShapeDtypeStruct((m, d_rnn), BF16), jax.ShapeDtypeStruct((n_seq, d_rnn), F32)],
        scratch_shapes=[pltpu.VMEM((SUBLANES, d_rnn), F32), pltpu.VMEM((tr, d_rnn), F32),
                        pltpu.VMEM((tr, d_rnn), F32), pltpu.VMEM((1, d_rnn), F32)],
        compiler_params=_cparams(("arbitrary",)),
        name="rglru",
    )(u, u, first, *states, h0, wc, bc, wg, ba, bx, lam, go)


def _prep_kernel(q_ref, k_ref, v_ref, lf_ref, c0_ref, qb_ref, kb_ref, vb_ref, cq_ref, ckt_ref, carry_s,
                 *, tm, n_kv, group):
    @pl.when(pl.program_id(0) == 0)
    def _():
        carry_s[...] = c0_ref[...]

    qb_ref[...] = (q_ref[...] * (HEAD_DIM ** -0.5)).astype(BF16)
    kb_ref[...] = k_ref[...].astype(BF16)
    vb_ref[...] = v_ref[...].astype(BF16)
    c = _split3_dot(_lower_tri(tm), lf_ref[...]) + carry_s[...]
    carry_s[...] = c[tm - 1:tm, :]
    ct = c.T
    for g in range(n_kv):
        h0 = g * group
        a0 = h0 // SUBLANES * SUBLANES
        rows8 = ct[a0:a0 + SUBLANES, :]
        ckt_ref[g] = rows8 if h0 == a0 else pltpu.roll(rows8, SUBLANES - (h0 - a0), axis=0)
        cq_ref[g] = c if g == 0 else pltpu.roll(c, LANES - h0, axis=1)


def _prep(u, lf, c0, *, tm, col_q, d_attn, n_kv, group):
    m = u.shape[0]
    dkv = n_kv * HEAD_DIM
    cq_blk, ck_blk, cv_blk = col_q // d_attn, (col_q + d_attn) // dkv, (col_q + d_attn + dkv) // dkv
    kern = functools.partial(_prep_kernel, tm=tm, n_kv=n_kv, group=group)
    return pl.pallas_call(
        kern,
        grid=(m // tm,),
        in_specs=[
            pl.BlockSpec((tm, d_attn), lambda i: (i, cq_blk)),
            pl.BlockSpec((tm, dkv), lambda i: (i, ck_blk)),
            pl.BlockSpec((tm, dkv), lambda i: (i, cv_blk)),
            pl.BlockSpec((tm, LANES), lambda i: (i, 0)),
            pl.BlockSpec((1, LANES), lambda i: (0, 0)),
        ],
        out_specs=[
            pl.BlockSpec((tm, d_attn), lambda i: (i, 0)),
            pl.BlockSpec((tm, dkv), lambda i: (i, 0)),
            pl.BlockSpec((tm, dkv), lambda i: (i, 0)),
            pl.BlockSpec((n_kv, tm, LANES), lambda i: (0, i, 0)),
            pl.BlockSpec((n_kv, SUBLANES, tm), lambda i: (0, 0, i)),
        ],
        out_shape=[
            jax.ShapeDtypeStruct((m, d_attn), BF16),
            jax.ShapeDtypeStruct((m, dkv), BF16),
            jax.ShapeDtypeStruct((m, dkv), BF16),
            jax.ShapeDtypeStruct((n_kv, m, LANES), F32),
            jax.ShapeDtypeStruct((n_kv, SUBLANES, m), F32),
        ],
        scratch_shapes=[pltpu.VMEM((1, LANES), F32)],
        compiler_params=_cparams(("arbitrary",)),
        name="attn_prep",
    )(u, u, u, lf, c0)


def _online_update(r, s, v, m_s, l_s, acc_s):
    m_prev = m_s[r]
    m_new = jnp.maximum(m_prev, jnp.max(s, axis=1, keepdims=True))
    alpha = jnp.exp(m_prev - m_new)
    p = jnp.exp(s - m_new)
    l_s[r] = alpha * l_s[r] + jnp.sum(p, axis=1, keepdims=True)
    acc_s[r] = alpha * acc_s[r] + jnp.dot(p.astype(BF16), v, preferred_element_type=F32)
    m_s[r] = m_new


def _qk(q, k):
    return lax.dot_general(q, k, (((1,), (1,)), ((), ())), preferred_element_type=F32)


def _attn_kernel(q_ref, k_ref, v_ref, cq_ref, ck_ref, km_ref, vm_ref, ckm_ref, o_ref, m_s, l_s, acc_s,
                 *, tq, tk, group, n_prefix):
    qi = pl.program_id(1)
    ki = pl.program_id(2)
    cq = cq_ref[...]

    @pl.when(ki == 0)
    def _():
        lane = lax.broadcasted_iota(jnp.int32, (tq, km_ref.shape[0]), 1)
        for r in range(group):
            m_s[r] = jnp.full((tq, 1), NEG_BIG, F32)
            l_s[r] = jnp.zeros((tq, 1), F32)
            acc_s[r] = jnp.zeros((tq, HEAD_DIM), F32)
            s = _qk(q_ref[:, r * HEAD_DIM:(r + 1) * HEAD_DIM], km_ref[...])
            s = s + (cq[:, r:r + 1] - ckm_ref[r:r + 1, :])
            s = jnp.where(lane < n_prefix, s, NEG_BIG)
            _online_update(r, s, vm_ref[...], m_s, l_s, acc_s)

    def step(masked):
        if masked:
            rows = qi * tq + lax.broadcasted_iota(jnp.int32, (tq, tk), 0)
            cols = ki * tk + lax.broadcasted_iota(jnp.int32, (tq, tk), 1)
            keep = cols <= rows
        for r in range(group):
            s = _qk(q_ref[:, r * HEAD_DIM:(r + 1) * HEAD_DIM], k_ref[...])
            s = s + (cq[:, r:r + 1] - ck_ref[r:r + 1, :])
            if masked:
                s = jnp.where(keep, s, NEG_BIG)
            _online_update(r, s, v_ref[...], m_s, l_s, acc_s)

    last = ((qi + 1) * tq - 1) // tk

    @pl.when(ki * tk + tk - 1 <= qi * tq)
    def _():
        step(False)

    @pl.when(jnp.logical_and(ki * tk + tk - 1 > qi * tq, ki <= last))
    def _():
        step(True)

    @pl.when(ki == last)
    def _():
        for r in range(group):
            o_ref[:, r * HEAD_DIM:(r + 1) * HEAD_DIM] = acc_s[r] / l_s[r]


def _attn(qb, kb, vb, cq, ckt, km, vm, ckm, *, tq, tk, n_kv, group, n_prefix):
    m, d_attn = qb.shape
    gw = group * HEAD_DIM
    npad = km.shape[0]

    def kv_map(g, qi, ki):
        return (jnp.minimum(ki, ((qi + 1) * tq - 1) // tk), g)

    def ck_map(g, qi, ki):
        return (g, 0, jnp.minimum(ki, ((qi + 1) * tq - 1) // tk))

    kern = functools.partial(_attn_kernel, tq=tq, tk=tk, group=group, n_prefix=n_prefix)
    return pl.pallas_call(
        kern,
        grid=(n_kv, m // tq, m // tk),
        in_specs=[
            pl.BlockSpec((tq, gw), lambda g, qi, ki: (qi, g)),
            pl.BlockSpec((tk, HEAD_DIM), kv_map),
            pl.BlockSpec((tk, HEAD_DIM), kv_map),
            pl.BlockSpec((None, tq, LANES), lambda g, qi, ki: (g, qi, 0)),
            pl.BlockSpec((None, SUBLANES, tk), ck_map),
            pl.BlockSpec((npad, HEAD_DIM), lambda g, qi, ki: (0, g)),
            pl.BlockSpec((npad, HEAD_DIM), lambda g, qi, ki: (0, g)),
            pl.BlockSpec((None, SUBLANES, npad), lambda g, qi, ki: (g, 0, 0)),
        ],
        out_specs=pl.BlockSpec((tq, gw), lambda g, qi, ki: (qi, g)),
        out_shape=jax.ShapeDtypeStruct((m, d_attn), F32),
        scratch_shapes=[pltpu.VMEM((group, tq, 1), F32), pltpu.VMEM((group, tq, 1), F32),
                        pltpu.VMEM((group, tq, HEAD_DIM), F32)],
        compiler_params=_cparams(("arbitrary", "arbitrary", "arbitrary")),
        name="fox_attn",
    )(qb, kb, vb, cq, ckt, km, vm, ckm)


LOG2E = 1.4426950408889634
AUG = 16
SKIP_LOG2 = 160.0


def _bf16_split3(x):
    hi = x.astype(BF16)
    r1 = x - hi.astype(F32)
    mid = r1.astype(BF16)
    lo = (r1 - mid.astype(F32)).astype(BF16)
    return hi, mid, lo


def _prep2_kernel(q_ref, k_ref, v_ref, lf_ref, c0_ref, qat_ref, ka_ref, vt_ref, st_ref, carry_s,
                  *, tm, n_kv, group):
    @pl.when(pl.program_id(0) == 0)
    def _():
        carry_s[...] = c0_ref[...]

    n_heads = n_kv * group
    c = _split3_dot(_lower_tri(tm), lf_ref[...]) + carry_s[...]
    carry_s[...] = c[tm - 1:tm, :]
    c2 = c * LOG2E
    splits = _bf16_split3(c2)
    splits_t = [s.astype(F32).T for s in splits]

    lane = lax.broadcasted_iota(jnp.int32, (1, LANES), 1)
    sub_aug = lax.broadcasted_iota(jnp.int32, (AUG, tm), 0)
    sel_r = lax.broadcasted_iota(jnp.int32, (LANES, LANES), 0)
    sel_c = lax.broadcasted_iota(jnp.int32, (LANES, LANES), 1)
    qstat = jnp.zeros((1, LANES), F32)
    kstat = jnp.zeros((1, LANES), F32)
    ones_cols = jnp.where(jnp.logical_and(lane >= 3 * group, lane < 3 * group + 3), 1.0, 0.0)

    for g in range(n_kv):
        kb = k_ref[:, g * HEAD_DIM:(g + 1) * HEAD_DIM].astype(BF16)
        ka_ref[g, 0, :, :HEAD_DIM] = kb
        aug = jnp.broadcast_to(ones_cols, (tm, LANES))
        for j in range(3):
            pick = jnp.logical_and(sel_r // group == g, sel_c == 3 * (sel_r % group) + j)
            aug = aug + jnp.dot(splits[j], jnp.where(pick, 1.0, 0.0).astype(BF16), preferred_element_type=F32)
        ka_ref[g, 0, :, HEAD_DIM:] = aug.astype(BF16)
        vt_ref[g, 0] = v_ref[:, g * HEAD_DIM:(g + 1) * HEAD_DIM].T.astype(BF16)
        kf = kb.astype(F32)
        kmax = jnp.max(jnp.sum(kf * kf, axis=1, keepdims=True), axis=0, keepdims=True)
        kstat = jnp.where(lane == g, kmax, kstat)
        for r in range(group):
            h = g * group + r
            qb = (q_ref[:, h * HEAD_DIM:(h + 1) * HEAD_DIM] * (HEAD_DIM ** -0.5 * LOG2E)).astype(BF16)
            qf = qb.astype(F32)
            qmax = jnp.max(jnp.sum(qf * qf, axis=1, keepdims=True), axis=0, keepdims=True)
            qstat = jnp.where(lane == h, qmax, qstat)
            qat_ref[h, :HEAD_DIM, :] = qf.T.astype(BF16)
            rows = jnp.where(jnp.logical_and(sub_aug >= 3 * r, sub_aug < 3 * r + 3), -1.0, 0.0)
            for j in range(3):
                rows = jnp.where(sub_aug == 3 * group + j, splits_t[j][h:h + 1, :], rows)
            qat_ref[h, HEAD_DIM:HEAD_DIM + AUG, :] = rows.astype(BF16)
            qat_ref[h, HEAD_DIM + AUG:, :] = jnp.zeros((HEAD_DIM - AUG, tm), BF16)
    del n_heads
    st_ref[0, 0:1, :] = qstat
    st_ref[0, 1:2, :] = kstat
    st_ref[0, 2:3, :] = c2[tm - 1:tm, :]
    st_ref[0, 3:4, :] = c2[0:1, :]
    st_ref[0, 4:, :] = jnp.zeros((SUBLANES - 4, LANES), F32)


def _prep2(u, lf, c0, *, tm, col_q, d_attn, n_kv, group):
    m = u.shape[0]
    nt = m // tm
    n_heads = n_kv * group
    dkv = n_kv * HEAD_DIM
    assert 3 * group + 3 <= AUG
    cq_blk, ck_blk, cv_blk = col_q // d_attn, (col_q + d_attn) // dkv, (col_q + d_attn + dkv) // dkv
    kern = functools.partial(_prep2_kernel, tm=tm, n_kv=n_kv, group=group)
    return pl.pallas_call(
        kern,
        grid=(nt,),
        in_specs=[
            pl.BlockSpec((tm, d_attn), lambda i: (i, cq_blk)),
            pl.BlockSpec((tm, dkv), lambda i: (i, ck_blk)),
            pl.BlockSpec((tm, dkv), lambda i: (i, cv_blk)),
            pl.BlockSpec((tm, LANES), lambda i: (i, 0)),
            pl.BlockSpec((1, LANES), lambda i: (0, 0)),
        ],
        out_specs=[
            pl.BlockSpec((n_heads, 2 * HEAD_DIM, tm), lambda i: (0, 0, i)),
            pl.BlockSpec((n_kv, 1, tm, 2 * HEAD_DIM), lambda i: (0, i, 0, 0)),
            pl.BlockSpec((n_kv, 1, HEAD_DIM, tm), lambda i: (0, i, 0, 0)),
            pl.BlockSpec((1, SUBLANES, LANES), lambda i: (i, 0, 0)),
        ],
        out_shape=[
            jax.ShapeDtypeStruct((n_heads, 2 * HEAD_DIM, m), BF16),
            jax.ShapeDtypeStruct((n_kv, nt, tm, 2 * HEAD_DIM), BF16),
            jax.ShapeDtypeStruct((n_kv, nt, HEAD_DIM, tm), BF16),
            jax.ShapeDtypeStruct((nt, SUBLANES, LANES), F32),
        ],
        scratch_shapes=[pltpu.VMEM((1, LANES), F32)],
        compiler_params=_cparams(("arbitrary",)),
        name="attn_prep",
    )(u, u, u, lf, c0)


def _attn2_kernel(nt_ref, pf_ref, qat_ref, ka_ref, vt_ref, kam_ref, vtm_ref, o_ref, sa_buf, sb_buf, *, t, nq):
    h = pl.program_id(0)
    qi = pl.program_id(1)
    q = qat_ref[...]

    def scores(ka):
        return jnp.dot(ka, q, preferred_element_type=F32)

    def absorb(s, smax, vt, carry):
        m_prev, l_prev, acc = carry
        m_new = jnp.maximum(m_prev, smax)
        alpha = jnp.exp2(m_prev - m_new)
        p = jnp.exp2(s - m_new)
        l_new = alpha * l_prev + jnp.sum(p, axis=0, keepdims=True)
        acc = alpha * acc + jnp.dot(vt, p.astype(BF16), preferred_element_type=F32)
        return m_new, l_new, acc

    carry = (jnp.full((1, t), NEG_BIG, F32), jnp.zeros((1, t), F32), jnp.zeros((HEAD_DIM, t), F32))
    key = lax.broadcasted_iota(jnp.int32, (t, t), 0)
    qry = lax.broadcasted_iota(jnp.int32, (t, t), 1)
    s_diag = jnp.where(key <= qry, scores(ka_ref[qi]), NEG_BIG)
    sa_buf[...] = s_diag
    n_older = nt_ref[h * nq + qi]
    colmax = lambda s: jnp.max(s, axis=0, keepdims=True)

    def step(k_next, src, dst, smax_src, carry):
        s_next = scores(ka_ref[k_next])
        dst[...] = s_next
        return colmax(s_next), absorb(src[...], smax_src, vt_ref[k_next + 1], carry)

    def older_pair(j, state):
        smax_a, carry = state
        k1 = qi - 1 - 2 * j
        smax_b, carry = step(k1, sa_buf, sb_buf, smax_a, carry)
        return step(k1 - 1, sb_buf, sa_buf, smax_b, carry)

    smax_a, carry = lax.fori_loop(0, n_older // 2, older_pair, (colmax(s_diag), carry))
    k_last = qi - n_older

    def odd_tail(carry):
        smax_b, carry = step(k_last, sa_buf, sb_buf, smax_a, carry)
        return absorb(sb_buf[...], smax_b, vt_ref[k_last], carry)

    carry = lax.cond(n_older % 2 == 1, odd_tail, lambda c: absorb(sa_buf[...], smax_a, vt_ref[k_last], c), carry)
    def prefix(carry):
        s_pre = scores(kam_ref[...])
        return absorb(s_pre, colmax(s_pre), vtm_ref[...], carry)

    _, l_fin, acc = lax.cond(pf_ref[h * nq + qi] == 1, prefix, lambda c: c, carry)
    o_ref[...] = (acc / l_fin).T


def _attn2(ntab, ptab, qat, ka, vt, kam, vtm, *, t, group):
    n_heads, _, m = qat.shape
    n_kv, nk = ka.shape[:2]
    nq = m // t
    npad = kam.shape[1]
    kern = functools.partial(_attn2_kernel, t=t, nq=nq)
    grid_spec = pltpu.PrefetchScalarGridSpec(
        num_scalar_prefetch=2,
        grid=(n_heads, nq),
        in_specs=[
            pl.BlockSpec((None, 2 * HEAD_DIM, t), lambda h, qi, nt, pf: (h, 0, qi)),
            pl.BlockSpec((None, nk, t, 2 * HEAD_DIM), lambda h, qi, nt, pf: (h // group, 0, 0, 0)),
            pl.BlockSpec((None, nk, HEAD_DIM, t), lambda h, qi, nt, pf: (h // group, 0, 0, 0)),
            pl.BlockSpec((None, npad, 2 * HEAD_DIM), lambda h, qi, nt, pf: (h // group, 0, 0)),
            pl.BlockSpec((None, HEAD_DIM, npad), lambda h, qi, nt, pf: (h // group, 0, 0)),
        ],
        out_specs=pl.BlockSpec((t, HEAD_DIM), lambda h, qi, nt, pf: (qi, h)),
        scratch_shapes=[pltpu.VMEM((t, t), F32), pltpu.VMEM((t, t), F32)],
    )
    return pl.pallas_call(
        kern,
        grid_spec=grid_spec,
        out_shape=jax.ShapeDtypeStruct((m, n_heads * HEAD_DIM), F32),
        compiler_params=_cparams(("arbitrary", "arbitrary")),
        name="fox_attn",
    )(ntab, ptab, qat, ka, vt, kam, vtm)


def _tile_schedule(stats, k2_prefix_max, c2_prefix_last, *, n_kv, group):
    n_heads = n_kv * group
    nt = stats.shape[0]
    qmax = jnp.sqrt(stats[:, 0, :n_heads])
    kmax = jnp.sqrt(jnp.maximum(jnp.max(stats[:, 1, :n_kv], axis=0), k2_prefix_max))
    smax = qmax * jnp.repeat(kmax, group)[None, :]
    c_last = stats[:, 2, :n_heads]
    c_first = stats[:, 3, :n_heads]
    bias_max = c_first[:, None, :] - c_last[None, :, :]
    needed = 2.0 * smax[:, None, :] + bias_max > -SKIP_LOG2
    qi = jnp.arange(nt)[:, None, None]
    kt = jnp.arange(nt)[None, :, None]
    dist = jnp.where(jnp.logical_and(needed, kt < qi), qi - kt, 0)
    n_older = jnp.max(dist, axis=1)
    prefix_needed = 2.0 * smax + (c_first - c2_prefix_last[None, :]) > -SKIP_LOG2
    flat = lambda a: a.T.astype(jnp.int32).reshape(-1)
    return flat(n_older), flat(prefix_needed)


def _attn_small_kernel(q_ref, k_ref, v_ref, lf_ref, ck_ref, cv_ref, clf_ref, o_ref, c_ref, ccum_s,
                       *, n_kv, group, n_cached_seqs, p_len, s_len):
    b = pl.program_id(0)
    has_cache = b < n_cached_seqs
    n_blk = p_len // LANES
    tri = _lower_tri(LANES)

    def cblock(j, off):
        r0 = pl.multiple_of(j * LANES, LANES)
        cb = _split3_dot(tri, clf_ref[pl.ds(r0, LANES), :]) + off
        ccum_s[pl.ds(r0, LANES), :] = cb
        return cb[LANES - 1:LANES, :]

    total = lax.fori_loop(0, n_blk, cblock, jnp.zeros((1, LANES), F32))
    total = jnp.where(has_cache, total, 0.0)
    zrows = jnp.zeros((LANES - s_len, LANES), F32)
    c_pad = _split3_dot(tri, jnp.concatenate([lf_ref[...], zrows], axis=0)) + total
    c_own = c_pad[:s_len, :]
    c_ref[...] = c_own
    c_own_t = c_pad.T
    c_cache_t = ccum_s[...].T

    rows = lax.broadcasted_iota(jnp.int32, (group * s_len, LANES), 0) % s_len
    cols = lax.broadcasted_iota(jnp.int32, (group * s_len, LANES), 1)
    scale = HEAD_DIM ** -0.5
    for g in range(n_kv):
        ksl = slice(g * HEAD_DIM, (g + 1) * HEAD_DIM)
        k_own = jnp.concatenate([k_ref[:, ksl], zrows], axis=0).astype(BF16)
        v_own = jnp.concatenate([v_ref[:, ksl], zrows], axis=0).astype(BF16)
        k_cache = ck_ref[:, ksl].astype(BF16)
        v_cache = cv_ref[:, ksl].astype(BF16)
        heads = range(g * group, (g + 1) * group)
        stack = lambda f: jnp.concatenate([f(h) for h in heads], axis=0)
        q = stack(lambda h: (q_ref[:, h * HEAD_DIM:(h + 1) * HEAD_DIM] * scale).astype(BF16))
        s_own = _qk(q, k_own) + stack(lambda h: c_own[:, h:h + 1] - c_own_t[h:h + 1, :])
        s_own = jnp.where(cols <= rows, s_own, NEG_BIG)
        s_cache = _qk(q, k_cache) + stack(lambda h: c_own[:, h:h + 1] - c_cache_t[h:h + 1, :])
        s_cache = jnp.where(has_cache, s_cache, NEG_BIG)
        mx = jnp.maximum(jnp.max(s_own, axis=1, keepdims=True), jnp.max(s_cache, axis=1, keepdims=True))
        p_own = jnp.exp(s_own - mx)
        p_cache = jnp.exp(s_cache - mx)
        den = jnp.sum(p_own, axis=1, keepdims=True) + jnp.sum(p_cache, axis=1, keepdims=True)
        num = jnp.dot(p_own.astype(BF16), v_own, preferred_element_type=F32)
        num = num + jnp.dot(p_cache.astype(BF16), v_cache, preferred_element_type=F32)
        out = num / den
        for r, h in enumerate(heads):
            o_ref[:, h * HEAD_DIM:(h + 1) * HEAD_DIM] = out[r * s_len:(r + 1) * s_len, :]


def _attn_small(u, lf, cache_k, cache_v, cache_lf, *, n_seq, s_len, col_q, d_attn, n_kv, group):
    n_cached, p_len, dkv = cache_k.shape
    ck_blk, cv_blk = (col_q + d_attn) // dkv, (col_q + d_attn + dkv) // dkv
    cq_blk = col_q // d_attn
    cmap = lambda b: (jnp.minimum(b, n_cached - 1), 0, 0)
    kern = functools.partial(_attn_small_kernel, n_kv=n_kv, group=group, n_cached_seqs=n_cached, p_len=p_len,
                             s_len=s_len)
    return pl.pallas_call(
        kern,
        grid=(n_seq,),
        in_specs=[
            pl.BlockSpec((s_len, d_attn), lambda b: (b, cq_blk)),
            pl.BlockSpec((s_len, dkv), lambda b: (b, ck_blk)),
            pl.BlockSpec((s_len, dkv), lambda b: (b, cv_blk)),
            pl.BlockSpec((s_len, LANES), lambda b: (b, 0)),
            pl.BlockSpec((None, p_len, dkv), cmap),
            pl.BlockSpec((None, p_len, dkv), cmap),
            pl.BlockSpec((None, p_len, LANES), cmap),
        ],
        out_specs=[pl.BlockSpec((s_len, d_attn), lambda b: (b, 0)),
                   pl.BlockSpec((s_len, LANES), lambda b: (b, 0))],
        out_shape=[jax.ShapeDtypeStruct((n_seq * s_len, d_attn), F32),
                   jax.ShapeDtypeStruct((n_seq * s_len, LANES), F32)],
        scratch_shapes=[pltpu.VMEM((p_len, LANES), F32)],
        compiler_params=_cparams(("arbitrary",)),
        name="fox_attn_small",
    )(u, u, u, lf, cache_k, cache_v, cache_lf)


def _outproj_kernel(orn_ref, oat_ref, ga_ref, w_ref, x_ref, h_ref, a_s, *, d_rnn):
    @pl.when(pl.program_id(1) == 0)
    def _():
        a_s[:, :d_rnn] = orn_ref[...]
        a_s[:, d_rnn:] = _rms(oat_ref[...], ga_ref[...]).astype(BF16)

    h_ref[...] = x_ref[...] + jnp.dot(a_s[...], w_ref[...], preferred_element_type=F32)


def _outproj(orn, oat, ga, w, x, *, tm, tn):
    m, d_rnn = orn.shape
    d_attn = oat.shape[1]
    d = x.shape[1]
    return pl.pallas_call(
        functools.partial(_outproj_kernel, d_rnn=d_rnn),
        grid=(m // tm, d // tn),
        in_specs=[
            pl.BlockSpec((tm, d_rnn), lambda i, j: (i, 0)),
            pl.BlockSpec((tm, d_attn), lambda i, j: (i, 0)),
            pl.BlockSpec((1, d_attn), lambda i, j: (0, 0)),
            pl.BlockSpec((d_rnn + d_attn, tn), lambda i, j: (0, j)),
            pl.BlockSpec((tm, tn), lambda i, j: (i, j)),
        ],
        out_specs=pl.BlockSpec((tm, tn), lambda i, j: (i, j)),
        out_shape=jax.ShapeDtypeStruct((m, d), F32),
        scratch_shapes=[pltpu.VMEM((tm, d_rnn + d_attn), BF16)],
        compiler_params=_cparams(("arbitrary", "arbitrary")),
        name="outproj",
    )(orn, oat, ga, w, x)


def _norm_cast_kernel(h_ref, g_ref, o_ref):
    o_ref[...] = _rms(h_ref[...], g_ref[...]).astype(BF16)


def _norm_cast(h, g, *, tm):
    m, d = h.shape
    return pl.pallas_call(
        _norm_cast_kernel,
        grid=(m // tm,),
        in_specs=[pl.BlockSpec((tm, d), lambda i: (i, 0)), pl.BlockSpec((1, d), lambda i: (0, 0))],
        out_specs=pl.BlockSpec((tm, d), lambda i: (i, 0)),
        out_shape=jax.ShapeDtypeStruct((m, d), BF16),
        compiler_params=_cparams(("arbitrary",)),
        name="norm_cast",
    )(h, g)


def _ffn_in_kernel(xn_ref, wg_ref, wv_ref, wc_ref, bc_ref, first_ref, p1_ref, p2_ref, act_ref, zt_ref,
                   carry_s, *, tm, seq_len, multi_seq, n_sub):
    i = pl.program_id(0)
    f = pl.program_id(1)

    @pl.when(i == 0)
    def _():
        carry_s[f] = first_ref[...]
    before = carry_s[f]

    assert n_sub == 1 or not multi_seq
    rs = tm // n_sub
    for r0 in range(0, tm, rs):
        xn = xn_ref[r0:r0 + rs, :]
        zg = jnp.dot(xn, wg_ref[...], preferred_element_type=F32)
        zv = jnp.dot(xn, wv_ref[...], preferred_element_type=F32)
        prev1, prev2 = _prev_rows(zg, before, (p1_ref, p2_ref), seq_len=seq_len, multi_seq=multi_seq)
        gate = bc_ref[...] + wc_ref[0:1, :] * prev2
        gate = gate + wc_ref[1:2, :] * prev1
        gate = gate + wc_ref[2:3, :] * zg
        before = zg[rs - SUBLANES:, :]
        if multi_seq:
            zt_ref[...] = zg
        act_ref[r0:r0 + rs, :] = (_gelu(gate) * zv).astype(BF16)
    if not multi_seq:
        carry_s[f] = before
        zt_ref[...] = before


def _ffn_in(xn, w, wc, bc, ffn_state, *, tm, tf, seq_len):
    m, d = xn.shape
    d_ff = w.shape[1] // 2
    nf = d_ff // tf
    multi_seq = m // seq_len > 1
    first, states = _conv_state_inputs(ffn_state, rows=tm, seq_len=seq_len, multi_seq=multi_seq)
    zt_rows = tm if multi_seq else SUBLANES
    st_spec = lambda a: pl.BlockSpec((a.shape[0], tf), lambda i, f: (0, f))
    sub_rows = 2 * LANES
    return pl.pallas_call(
        functools.partial(_ffn_in_kernel, tm=tm, seq_len=seq_len, multi_seq=multi_seq,
                          n_sub=tm // sub_rows if tm % sub_rows == 0 else 1),
        grid=(m // tm, nf),
        in_specs=[
            pl.BlockSpec((tm, d), lambda i, f: (i, 0)),
            pl.BlockSpec((d, tf), lambda i, f: (0, f)),
            pl.BlockSpec((d, tf), lambda i, f: (0, f + nf)),
            pl.BlockSpec((wc.shape[0], tf), lambda i, f: (0, f)),
            pl.BlockSpec((1, tf), lambda i, f: (0, f)),
            st_spec(first)] + [st_spec(s) for s in states],
        out_specs=[pl.BlockSpec((tm, tf), lambda i, f: (i, f)),
                   pl.BlockSpec((zt_rows, tf), lambda i, f: (i, f))],
        out_shape=[jax.ShapeDtypeStruct((m, d_ff), BF16),
                   jax.ShapeDtypeStruct((m // tm * zt_rows, d_ff), F32)],
        scratch_shapes=[pltpu.VMEM((nf, SUBLANES, tf), F32)],
        compiler_params=_cparams(("arbitrary", "arbitrary")),
        name="ffn_in",
    )(xn, w, w, wc, bc, first, *states)


def _ffn_out_kernel(act_ref, w_ref, h_ref, g_ref, y_ref, *, tc):
    k = pl.program_id(1)
    d = y_ref.shape[1]
    chunks = [slice(c, c + tc) for c in range(0, d, tc)]

    @pl.when(k == 0)
    def _():
        y_ref[...] = h_ref[...]

    act = act_ref[...]
    for sl in chunks:
        y_ref[:, sl] += jnp.dot(act, w_ref[:, sl], preferred_element_type=F32)

    @pl.when(k == pl.num_programs(1) - 1)
    def _():
        ssq = jnp.zeros((y_ref.shape[0], 1), F32)
        for sl in chunks:
            yc = y_ref[:, sl]
            ssq = ssq + jnp.sum(yc * yc, axis=-1, keepdims=True)
        inv = lax.rsqrt(ssq / d + EPS)
        for sl in chunks:
            y_ref[:, sl] = y_ref[:, sl] * inv * g_ref[:, sl]


def _ffn_out(act, w, h, g, *, tm, tk):
    m, d_ff = act.shape
    d = h.shape[1]
    return pl.pallas_call(
        functools.partial(_ffn_out_kernel, tc=_pick(d, 1024)),
        grid=(m // tm, d_ff // tk),
        in_specs=[
            pl.BlockSpec((tm, tk), lambda i, k: (i, k)),
            pl.BlockSpec((tk, d), lambda i, k: (k, 0)),
            pl.BlockSpec((tm, d), lambda i, k: (i, 0)),
            pl.BlockSpec((1, d), lambda i, k: (0, 0)),
        ],
        out_specs=pl.BlockSpec((tm, d), lambda i, k: (i, 0)),
        out_shape=jax.ShapeDtypeStruct((m, d), F32),
        compiler_params=_cparams(("arbitrary", "arbitrary")),
        name="ffn_out",
    )(act, w, h, g)


def _pick(n, pref):
    t = min(n, pref)
    while n % t:
        t //= 2
    return t


def _place_state(state, seq_len, back):
    n_seq, w, c = state.shape
    out = jnp.zeros((n_seq, seq_len, c), state.dtype)
    out = out.at[:, :back, :].set(state[:, w - back:, :])
    return out.reshape(n_seq * seq_len, c)


def kernel(x_prompt, x_sample, cache_k, cache_v, cache_logf, state_rnn_h, state_rnn_conv, state_ffn_conv,
           meta_tokens, g_mix, w_in, b_f, w_rnn_conv, b_rnn_conv, w_rg_a, b_rg_a, w_rg_x, b_rg_x, lru_lambda,
           g_out_rnn, g_out_attn, w_out, g_ffn, w_ffn_in, w_ffn_conv, b_ffn_conv, w_ffn_out, g_final):
    depth = g_mix.shape[0]
    assert depth == 1, "single-layer stack"
    batch, seq, d_model = x_prompt.shape
    assert batch == 1
    dec_batch, dec_seq, _ = x_sample.shape
    n_meta = meta_tokens.shape[0]
    assert n_meta == dec_seq, "meta prefix is run as one more sample-length sequence"
    n_heads = b_f.shape[1]
    n_kv = cache_k.shape[3]
    group = n_heads // n_kv
    assert group <= SUBLANES and cache_k.shape[4] == HEAD_DIM
    d_attn = n_heads * HEAD_DIM
    d_rnn = state_rnn_h.shape[2]
    dkv = n_kv * HEAD_DIM
    d_ff = state_ffn_conv.shape[3]
    past = cache_k.shape[2]
    col_q = 2 * d_rnn
    n_main_cols = col_q + d_attn + 2 * dkv
    n_rnn_blocks = w_rg_a.shape[1]
    rnn_w = w_rnn_conv.shape[1]
    ffn_w = w_ffn_conv.shape[1]
    assert rnn_w == 4 and ffn_w == 3

    w_in_b = w_in[0][:, :n_main_cols].astype(BF16)
    wf_b = jnp.pad(w_in[0][:, n_main_cols:], ((0, 0), (0, LANES - n_heads))).astype(BF16)
    bf_p = jnp.pad(b_f[0], (0, LANES - n_heads)).reshape(1, LANES)
    w_out_b = w_out[0].astype(BF16)
    w_ffn_in_b = w_ffn_in[0].astype(BF16)
    w_ffn_out_b = w_ffn_out[0].astype(BF16)
    wg = jnp.concatenate([w_rg_a[0], w_rg_x[0]], axis=-1).astype(BF16)
    row = lambda a: a.reshape(1, -1)
    rg_args = (w_rnn_conv[0], row(b_rnn_conv[0]), wg, row(b_rg_a[0]), row(b_rg_x[0]), row(lru_lambda[0]),
               row(g_out_rnn[0]))

    def chain(x, seq_len, rnn_state, h0, ffn_state, attn_fn, tm, tm_ffn, tr, tn, tf, tk):
        u, lf = _inproj(x, row(g_mix[0]), w_in_b, wf_b, bf_p, tm=tm, tn=tn)
        orn, h_last = _rglru(u, rnn_state, h0, *rg_args, tr=tr, seq_len=seq_len, d_rnn=d_rnn)
        oat, attn_aux = attn_fn(u, lf)
        h1 = _outproj(orn, oat, row(g_out_attn[0]), w_out_b, x, tm=tm, tn=_pick(d_model, tn))
        xn = _norm_cast(h1, row(g_ffn[0]), tm=tm)
        act, zt = _ffn_in(xn, w_ffn_in_b, w_ffn_conv[0], row(b_ffn_conv[0]), ffn_state,
                          tm=tm_ffn, tf=tf, seq_len=seq_len)
        y = _ffn_out(act, w_ffn_out_b, h1, row(g_final), tm=tm, tk=tk)
        return u, lf, h_last, zt, y, attn_aux

    n_small = dec_batch + 1
    ms = n_small * dec_seq
    xs = jnp.concatenate([x_sample.reshape(dec_batch * dec_seq, d_model), meta_tokens.astype(F32)], axis=0)
    zpad = lambda a: jnp.concatenate([a, jnp.zeros((1,) + a.shape[1:], a.dtype)], axis=0)
    clf = jnp.pad(cache_logf[0], ((0, 0), (0, 0), (0, LANES - n_heads)))

    def small_attn(u, lf):
        return _attn_small(u, lf, cache_k[0].reshape(dec_batch, past, dkv), cache_v[0].reshape(dec_batch, past, dkv),
                           clf, n_seq=n_small, s_len=dec_seq, col_q=col_q, d_attn=d_attn, n_kv=n_kv, group=group)

    tn, tf, tk = _pick(n_main_cols, 1024), _pick(d_ff, 512), _pick(d_ff, 1024)
    u_s, lf_s, hl_s, zt_s, y_s, c_s = chain(
        xs, dec_seq, zpad(state_rnn_conv[0]), zpad(state_rnn_h[0]), zpad(state_ffn_conv[0]), small_attn,
        tm=ms, tm_ffn=ms, tr=ms, tn=tn, tf=tf, tk=tk)

    m0 = dec_batch * dec_seq
    u_meta = u_s[m0:]
    k_meta = u_meta[:, col_q + d_attn:col_q + d_attn + dkv]
    v_meta = u_meta[:, col_q + d_attn + dkv:]
    c_meta = c_s[m0:]
    npad = -(-n_meta // 16) * 16
    c2m = (c_meta[:, :n_heads] * LOG2E).reshape(n_meta, n_kv, group).transpose(1, 0, 2)
    c2m = jnp.pad(c2m, ((0, 0), (0, npad - n_meta), (0, 0)), constant_values=-NEG_BIG)
    bias_cols = jnp.stack(_bf16_split3(c2m), axis=-1).reshape(n_kv, npad, 3 * group)
    bias_cols = jnp.concatenate([bias_cols, jnp.ones((n_kv, npad, 3), BF16),
                                 jnp.zeros((n_kv, npad, HEAD_DIM - 3 * group - 3), BF16)], axis=-1)
    k_meta_p = jnp.pad(k_meta, ((0, npad - n_meta), (0, 0))).reshape(npad, n_kv, HEAD_DIM).transpose(1, 0, 2)
    kam = jnp.concatenate([k_meta_p.astype(BF16), bias_cols], axis=-1)
    vtm = jnp.pad(v_meta, ((0, npad - n_meta), (0, 0))).reshape(npad, n_kv, HEAD_DIM).transpose(1, 2, 0)
    vtm = vtm.astype(BF16)
    tq = _pick(seq, 512)

    def main_attn(u, lf):
        qat, ka, vt, stats = _prep2(u, lf, c_meta[n_meta - 1:], tm=tq, col_q=col_q, d_attn=d_attn, n_kv=n_kv,
                                    group=group)
        k_meta_f = k_meta.astype(BF16).astype(F32).reshape(n_meta, n_kv, HEAD_DIM)
        ntab, ptab = _tile_schedule(stats, jnp.max(jnp.sum(k_meta_f * k_meta_f, axis=-1), axis=0),
                                    c_meta[n_meta - 1, :n_heads] * LOG2E, n_kv=n_kv, group=group)
        return _attn2(ntab, ptab, qat, ka, vt, kam, vtm, t=tq, group=group), None

    u_m, lf_m, hl_m, zt_m, y_m, _ = chain(
        x_prompt[0], seq, u_meta[None, n_meta - (rnn_w - 1):, :d_rnn], hl_s[dec_batch:],
        zt_s[None, ms - (ffn_w - 1):, :], main_attn,
        tm=_pick(seq, 512), tm_ffn=_pick(seq, 1024), tr=_pick(seq, 256), tn=tn, tf=tf, tk=tk)

    kcols = slice(col_q + d_attn, col_q + d_attn + dkv)
    vcols = slice(col_q + d_attn + dkv, n_main_cols)
    y_prompt = y_m[None]
    y_sample = y_s[:m0].reshape(dec_batch, dec_seq, d_model)
    new_k_prompt = jnp.concatenate([k_meta, u_m[:, kcols]], axis=0).reshape(1, 1, n_meta + seq, n_kv, HEAD_DIM)
    new_v_prompt = jnp.concatenate([v_meta, u_m[:, vcols]], axis=0).reshape(1, 1, n_meta + seq, n_kv, HEAD_DIM)
    new_logf_prompt = jnp.concatenate([lf_s[m0:, :n_heads], lf_m[:, :n_heads]], axis=0)[None, None]
    new_rnn_h_prompt = hl_m[None]
    new_rnn_conv_prompt = u_m[seq - (rnn_w - 1):, :d_rnn][None, None]
    new_ffn_conv_prompt = zt_m[zt_m.shape[0] - (ffn_w - 1):][None, None]
    us3 = u_s[:m0].reshape(dec_batch, dec_seq, n_main_cols)
    new_k_sample = us3[:, :, kcols].reshape(1, dec_batch, dec_seq, n_kv, HEAD_DIM)
    new_v_sample = us3[:, :, vcols].reshape(1, dec_batch, dec_seq, n_kv, HEAD_DIM)
    new_logf_sample = lf_s[:m0, :n_heads].reshape(1, dec_batch, dec_seq, n_heads)
    new_rnn_h_sample = hl_s[:dec_batch][None]
    new_rnn_conv_sample = us3[:, dec_seq - (rnn_w - 1):, :d_rnn][None]
    new_ffn_conv_sample = zt_s[:m0].reshape(dec_batch, dec_seq, d_ff)[:, dec_seq - (ffn_w - 1):][None]
    return (y_prompt, y_sample, new_k_prompt, new_v_prompt, new_logf_prompt, new_rnn_h_prompt,
            new_rnn_conv_prompt, new_ffn_conv_prompt, new_k_sample, new_v_sample, new_logf_sample,
            new_rnn_h_sample, new_rnn_conv_sample, new_ffn_conv_sample)
```

```python
import functools

import jax
import jax.numpy as jnp
from jax import lax
from jax.experimental import pallas as pl
from jax.experimental.pallas import tpu as pltpu

F32 = jnp.float32
BF16 = jnp.bfloat16

EPS = 1e-6
LRU_C = 8.0
HEAD_DIM = 128
LANES = 128
SUBLANES = 8
NEG_BIG = -1e30
VMEM_LIMIT = 56 * 1024 * 1024


def _cparams(sem):
    return pltpu.CompilerParams(dimension_semantics=sem, vmem_limit_bytes=VMEM_LIMIT)


def _rms(x, g):
    return x * lax.rsqrt(jnp.mean(x * x, axis=-1, keepdims=True) + EPS) * g


def _softplus(x):
    return jnp.maximum(x, 0.0) + jnp.log1p(jnp.exp(-jnp.abs(x)))


def _gelu(x):
    return jax.nn.gelu(x, approximate=True)


def _sigmoid(x):
    return 0.5 * jnp.tanh(0.5 * x) + 0.5


def _split3_dot(l_bf16, x):
    hi = x.astype(BF16)
    r1 = x - hi.astype(F32)
    mid = r1.astype(BF16)
    lo = (r1 - mid.astype(F32)).astype(BF16)
    out = jnp.dot(l_bf16, hi, preferred_element_type=F32)
    out = out + jnp.dot(l_bf16, mid, preferred_element_type=F32)
    return out + jnp.dot(l_bf16, lo, preferred_element_type=F32)


def _lower_tri(n):
    r = lax.broadcasted_iota(jnp.int32, (n, n), 0)
    c = lax.broadcasted_iota(jnp.int32, (n, n), 1)
    return jnp.where(c <= r, 1.0, 0.0).astype(BF16)


def _inproj_kernel(x_ref, g_ref, w_ref, wf_ref, bf_ref, u_ref, lf_ref, xn_ref):
    j = pl.program_id(1)
    tm = x_ref.shape[0]
    rs = 2 * LANES if tm % (2 * LANES) == 0 else tm

    @pl.when(j == 0)
    def _():
        for r0 in range(0, tm, rs):
            xb = _rms(x_ref[r0:r0 + rs, :], g_ref[...]).astype(BF16)
            xn_ref[r0:r0 + rs, :] = xb
            f = jnp.dot(xb, wf_ref[...], preferred_element_type=F32) + bf_ref[...]
            lf_ref[r0:r0 + rs, :] = -_softplus(-f)
            u_ref[r0:r0 + rs, :] = jnp.dot(xb, w_ref[...], preferred_element_type=F32)

    @pl.when(j > 0)
    def _():
        u_ref[...] = jnp.dot(xn_ref[...], w_ref[...], preferred_element_type=F32)


def _inproj(x, g, w, wf, bf, *, nc, tm, tn):
    m, d = x.shape
    assert nc % tn == 0 and nc <= w.shape[1]
    return pl.pallas_call(
        _inproj_kernel,
        grid=(m // tm, nc // tn),
        in_specs=[
            pl.BlockSpec((tm, d), lambda i, j: (i, 0)),
            pl.BlockSpec((1, d), lambda i, j: (0, 0)),
            pl.BlockSpec((d, tn), lambda i, j: (0, j)),
            pl.BlockSpec((d, LANES), lambda i, j: (0, 0)),
            pl.BlockSpec((1, LANES), lambda i, j: (0, 0)),
        ],
        out_specs=[
            pl.BlockSpec((tm, tn), lambda i, j: (i, j)),
            pl.BlockSpec((tm, LANES), lambda i, j: (i, 0)),
        ],
        out_shape=[jax.ShapeDtypeStruct((m, nc), F32), jax.ShapeDtypeStruct((m, LANES), F32)],
        scratch_shapes=[pltpu.VMEM((tm, d), BF16)],
        compiler_params=_cparams(("arbitrary", "arbitrary")),
        name="inproj",
    )(x, g, w, wf, bf)


def _prev_rows(x, before, states, *, seq_len, multi_seq):
    rows, c = x.shape
    prevs = []
    for k in range(1, len(states) + 1):
        rolled = pltpu.roll(x, k, axis=0)
        if multi_seq:
            pos = lax.broadcasted_iota(jnp.int32, (rows, 1), 0) % seq_len
            pk = jnp.where(pos >= k, rolled, states[k - 1][...])
        else:
            sub = lax.broadcasted_iota(jnp.int32, (SUBLANES, c), 0)
            head = jnp.where(sub < k, pltpu.roll(before, k, axis=0), rolled[:SUBLANES, :])
            pk = jnp.concatenate([head, rolled[SUBLANES:, :]], axis=0)
        prevs.append(pk)
    return prevs


def _rglru_kernel(xr_ref, yg_ref, first_ref, p1_ref, p2_ref, p3_ref, h0_ref, wc_ref, bc_ref, wg_ref, ba_ref,
                  bx_ref, lam_ref, go_ref, o_ref, hl_ref, xp_s, a_s, b_s, h_s, *, tr, seq_len, n_blocks,
                  multi_seq):
    i = pl.program_id(0)
    d = xr_ref.shape[1]
    blk = d // n_blocks
    x = xr_ref[...]

    @pl.when(i == 0)
    def _():
        h_s[...] = jnp.zeros_like(h_s)
        xp_s[...] = first_ref[...]

    prev1, prev2, prev3 = _prev_rows(x, xp_s[...], (p1_ref, p2_ref, p3_ref), seq_len=seq_len,
                                     multi_seq=multi_seq)
    xc = bc_ref[...] + wc_ref[0:1, :] * prev3
    xc = xc + wc_ref[1:2, :] * prev2
    xc = xc + wc_ref[2:3, :] * prev1
    xc = xc + wc_ref[3:4, :] * x
    xp_s[...] = x[tr - SUBLANES:, :]

    sp = _softplus(-lam_ref[...])
    xcb = xc.astype(BF16)
    for n in range(n_blocks):
        sl = slice(n * blk, (n + 1) * blk)
        gn = jnp.dot(xcb[:, sl], wg_ref[n], preferred_element_type=F32)
        r = _sigmoid(gn[:, :blk] + ba_ref[:, sl])
        ig = _sigmoid(gn[:, blk:] + bx_ref[:, sl])
        log_a = (-LRU_C) * r * sp[:, sl]
        a = jnp.exp(log_a)
        a_s[:, sl] = a
        y = -jnp.tanh(log_a) * (a * a + 1.0)
        root = jnp.where(y > 0.0, y * lax.rsqrt(y), 0.0)
        b_s[:, sl] = root * (ig * xc[:, sl])

    row = lax.broadcasted_iota(jnp.int32, (SUBLANES, d), 0)
    groups_per_seq = seq_len // SUBLANES

    def group(gi, h):
        r0 = pl.multiple_of(gi * SUBLANES, SUBLANES)
        gg = i * (tr // SUBLANES) + gi
        seq = gg // groups_per_seq
        first = (gg % groups_per_seq) == 0
        h = jnp.where(first, h0_ref[pl.ds(seq, 1), :], h)
        a8 = a_s[pl.ds(r0, SUBLANES), :]
        b8 = b_s[pl.ds(r0, SUBLANES), :]
        for sh in (1, 2, 4):
            ok = row >= sh
            b8 = jnp.where(ok, a8 * pltpu.roll(b8, sh, axis=0) + b8, b8)
            a8 = jnp.where(ok, a8 * pltpu.roll(a8, sh, axis=0), a8)
        h8 = a8 * h + b8
        b_s[pl.ds(r0, SUBLANES), :] = h8
        hn = h8[SUBLANES - 1:SUBLANES, :]

        @pl.when((gg % groups_per_seq) == groups_per_seq - 1)
        def _():
            hl_ref[pl.ds(seq, 1), :] = hn
        return hn

    h_s[...] = lax.fori_loop(0, tr // SUBLANES, group, h_s[...])

    o = b_s[...] * _gelu(yg_ref[...])
    o_ref[...] = _rms(o, go_ref[...]).astype(BF16)


def _conv_state_inputs(state, *, rows, seq_len, multi_seq):
    n_seq, w, c = state.shape
    dummy = jnp.zeros((SUBLANES, c), F32)
    if multi_seq:
        assert rows == n_seq * seq_len
        first = dummy
        states = [_place_state(state, seq_len, k) for k in range(1, w + 1)]
    else:
        assert n_seq == 1
        first = jnp.pad(state[0], ((SUBLANES - w, 0), (0, 0)))
        states = [dummy] * w
    return first, states


def _rglru(u, rnn_state, h0, wc, bc, wg, ba, bx, lam, go, *, tr, seq_len, d_rnn):
    m = u.shape[0]
    n_seq = m // seq_len
    multi_seq = n_seq > 1
    n_blocks = wg.shape[0]
    first, states = _conv_state_inputs(rnn_state, rows=tr, seq_len=seq_len, multi_seq=multi_seq)
    row_spec = lambda c: pl.BlockSpec((tr, d_rnn), lambda i: (i, c))
    full = lambda a: pl.BlockSpec(a.shape, lambda i: (0,) * a.ndim)
    kern = functools.partial(_rglru_kernel, tr=tr, seq_len=seq_len, n_blocks=n_blocks, multi_seq=multi_seq)
    return pl.pallas_call(
        kern,
        grid=(m // tr,),
        in_specs=[row_spec(0), row_spec(1), full(first)] + [full(s) for s in states] + [
            full(h0), full(wc), full(bc), full(wg), full(ba), full(bx), full(lam), full(go)],
        out_specs=[pl.BlockSpec((tr, d_rnn), lambda i: (i, 0)), pl.BlockSpec((n_seq, d_rnn), lambda i: (0, 0))],
        out_shape=[jax.ShapeDtypeStruct((m, d_rnn), BF16), jax.ShapeDtypeStruct((n_seq, d_rnn), F32)],
        scratch_shapes=[pltpu.VMEM((SUBLANES, d_rnn), F32), pltpu.VMEM((tr, d_rnn), F32),
                        pltpu.VMEM((tr, d_rnn), F32), pltpu.VMEM((1, d_rnn), F32)],
        compiler_params=_cparams(("arbitrary",)),
        name="rglru",
    )(u, u, first, *states, h0, wc, bc, wg, ba, bx, lam, go)


def _qk(q, k):
    return lax.dot_general(q, k, (((1,), (1,)), ((), ())), preferred_element_type=F32)


LOG2E = 1.4426950408889634
AUG = 16
SKIP_LOG2 = 160.0


def _bf16_split3(x):
    hi = x.astype(BF16)
    r1 = x - hi.astype(F32)
    mid = r1.astype(BF16)
    lo = (r1 - mid.astype(F32)).astype(BF16)
    return hi, mid, lo


def _prep2_kernel(q_ref, k_ref, v_ref, lf_ref, c0_ref, qat_ref, ka_ref, vt_ref, st_ref, carry_s,
                  *, tm, n_kv, group):
    @pl.when(pl.program_id(0) == 0)
    def _():
        carry_s[...] = c0_ref[...]

    n_heads = n_kv * group
    c = _split3_dot(_lower_tri(tm), lf_ref[...]) + carry_s[...]
    carry_s[...] = c[tm - 1:tm, :]
    c2 = c * LOG2E
    splits = _bf16_split3(c2)
    splits_t = [s.astype(F32).T for s in splits]

    lane = lax.broadcasted_iota(jnp.int32, (1, LANES), 1)
    sub_aug = lax.broadcasted_iota(jnp.int32, (AUG, tm), 0)
    sel_r = lax.broadcasted_iota(jnp.int32, (LANES, LANES), 0)
    sel_c = lax.broadcasted_iota(jnp.int32, (LANES, LANES), 1)
    qstat = jnp.zeros((1, LANES), F32)
    kstat = jnp.zeros((1, LANES), F32)
    ones_cols = jnp.where(jnp.logical_and(lane >= 3 * group, lane < 3 * group + 3), 1.0, 0.0)

    for g in range(n_kv):
        kb = k_ref[:, g * HEAD_DIM:(g + 1) * HEAD_DIM].astype(BF16)
        ka_ref[g, 0, :, :HEAD_DIM] = kb
        aug = jnp.broadcast_to(ones_cols, (tm, LANES))
        for j in range(3):
            pick = jnp.logical_and(sel_r // group == g, sel_c == 3 * (sel_r % group) + j)
            aug = aug + jnp.dot(splits[j], jnp.where(pick, 1.0, 0.0).astype(BF16), preferred_element_type=F32)
        ka_ref[g, 0, :, HEAD_DIM:] = aug.astype(BF16)
        vt_ref[g, 0] = v_ref[:, g * HEAD_DIM:(g + 1) * HEAD_DIM].T.astype(BF16)
        kf = kb.astype(F32)
        kmax = jnp.max(jnp.sum(kf * kf, axis=1, keepdims=True), axis=0, keepdims=True)
        kstat = jnp.where(lane == g, kmax, kstat)
        for r in range(group):
            h = g * group + r
            qb = (q_ref[:, h * HEAD_DIM:(h + 1) * HEAD_DIM] * (HEAD_DIM ** -0.5 * LOG2E)).astype(BF16)
            qf = qb.astype(F32)
            qmax = jnp.max(jnp.sum(qf * qf, axis=1, keepdims=True), axis=0, keepdims=True)
            qstat = jnp.where(lane == h, qmax, qstat)
            qat_ref[h, :HEAD_DIM, :] = qf.T.astype(BF16)
            rows = jnp.where(jnp.logical_and(sub_aug >= 3 * r, sub_aug < 3 * r + 3), -1.0, 0.0)
            for j in range(3):
                rows = jnp.where(sub_aug == 3 * group + j, splits_t[j][h:h + 1, :], rows)
            qat_ref[h, HEAD_DIM:HEAD_DIM + AUG, :] = rows.astype(BF16)
            qat_ref[h, HEAD_DIM + AUG:, :] = jnp.zeros((HEAD_DIM - AUG, tm), BF16)
    del n_heads
    st_ref[0, 0:1, :] = qstat
    st_ref[0, 1:2, :] = kstat
    st_ref[0, 2:3, :] = c2[tm - 1:tm, :]
    st_ref[0, 3:4, :] = c2[0:1, :]
    st_ref[0, 4:, :] = jnp.zeros((SUBLANES - 4, LANES), F32)


def _prep2(u, lf, c0, *, tm, col_q, d_attn, n_kv, group):
    m = u.shape[0]
    nt = m // tm
    n_heads = n_kv * group
    dkv = n_kv * HEAD_DIM
    assert 3 * group + 3 <= AUG
    cq_blk, ck_blk, cv_blk = col_q // d_attn, (col_q + d_attn) // dkv, (col_q + d_attn + dkv) // dkv
    kern = functools.partial(_prep2_kernel, tm=tm, n_kv=n_kv, group=group)
    return pl.pallas_call(
        kern,
        grid=(nt,),
        in_specs=[
            pl.BlockSpec((tm, d_attn), lambda i: (i, cq_blk)),
            pl.BlockSpec((tm, dkv), lambda i: (i, ck_blk)),
            pl.BlockSpec((tm, dkv), lambda i: (i, cv_blk)),
            pl.BlockSpec((tm, LANES), lambda i: (i, 0)),
            pl.BlockSpec((1, LANES), lambda i: (0, 0)),
        ],
        out_specs=[
            pl.BlockSpec((n_heads, 2 * HEAD_DIM, tm), lambda i: (0, 0, i)),
            pl.BlockSpec((n_kv, 1, tm, 2 * HEAD_DIM), lambda i: (0, i, 0, 0)),
            pl.BlockSpec((n_kv, 1, HEAD_DIM, tm), lambda i: (0, i, 0, 0)),
            pl.BlockSpec((1, SUBLANES, LANES), lambda i: (i, 0, 0)),
        ],
        out_shape=[
            jax.ShapeDtypeStruct((n_heads, 2 * HEAD_DIM, m), BF16),
            jax.ShapeDtypeStruct((n_kv, nt, tm, 2 * HEAD_DIM), BF16),
            jax.ShapeDtypeStruct((n_kv, nt, HEAD_DIM, tm), BF16),
            jax.ShapeDtypeStruct((nt, SUBLANES, LANES), F32),
        ],
        scratch_shapes=[pltpu.VMEM((1, LANES), F32)],
        compiler_params=_cparams(("arbitrary",)),
        name="attn_prep",
    )(u, u, u, lf, c0)


def _attn2_kernel(nt_ref, pf_ref, qat_ref, ka_ref, vt_ref, kam_ref, vtm_ref, o_ref, sa_buf, sb_buf, *, t, nq):
    h = pl.program_id(0)
    qi = pl.program_id(1)
    q = qat_ref[...]

    def scores(ka):
        return jnp.dot(ka, q, preferred_element_type=F32)

    def absorb(s, smax, vt, carry):
        m_prev, l_prev, acc = carry
        m_new = jnp.maximum(m_prev, smax)
        alpha = jnp.exp2(m_prev - m_new)
        p = jnp.exp2(s - m_new)
        l_new = alpha * l_prev + jnp.sum(p, axis=0, keepdims=True)
        acc = alpha * acc + jnp.dot(vt, p.astype(BF16), preferred_element_type=F32)
        return m_new, l_new, acc

    carry = (jnp.full((1, t), NEG_BIG, F32), jnp.zeros((1, t), F32), jnp.zeros((HEAD_DIM, t), F32))
    key = lax.broadcasted_iota(jnp.int32, (t, t), 0)
    qry = lax.broadcasted_iota(jnp.int32, (t, t), 1)
    s_diag = jnp.where(key <= qry, scores(ka_ref[qi]), NEG_BIG)
    sa_buf[...] = s_diag
    n_older = nt_ref[h * nq + qi]
    colmax = lambda s: jnp.max(s, axis=0, keepdims=True)

    def step(k_next, src, dst, smax_src, carry):
        s_next = scores(ka_ref[k_next])
        dst[...] = s_next
        return colmax(s_next), absorb(src[...], smax_src, vt_ref[k_next + 1], carry)

    def older_pair(j, state):
        smax_a, carry = state
        k1 = qi - 1 - 2 * j
        smax_b, carry = step(k1, sa_buf, sb_buf, smax_a, carry)
        return step(k1 - 1, sb_buf, sa_buf, smax_b, carry)

    smax_a, carry = lax.fori_loop(0, n_older // 2, older_pair, (colmax(s_diag), carry))
    k_last = qi - n_older

    def odd_tail(carry):
        smax_b, carry = step(k_last, sa_buf, sb_buf, smax_a, carry)
        return absorb(sb_buf[...], smax_b, vt_ref[k_last], carry)

    carry = lax.cond(n_older % 2 == 1, odd_tail, lambda c: absorb(sa_buf[...], smax_a, vt_ref[k_last], c), carry)
    def prefix(carry):
        s_pre = scores(kam_ref[...])
        return absorb(s_pre, colmax(s_pre), vtm_ref[...], carry)

    _, l_fin, acc = lax.cond(pf_ref[h * nq + qi] == 1, prefix, lambda c: c, carry)
    o_ref[...] = (acc / l_fin).T


def _attn2(ntab, ptab, qat, ka, vt, kam, vtm, *, t, group):
    n_heads, _, m = qat.shape
    n_kv, nk = ka.shape[:2]
    nq = m // t
    npad = kam.shape[1]
    kern = functools.partial(_attn2_kernel, t=t, nq=nq)
    grid_spec = pltpu.PrefetchScalarGridSpec(
        num_scalar_prefetch=2,
        grid=(n_heads, nq),
        in_specs=[
            pl.BlockSpec((None, 2 * HEAD_DIM, t), lambda h, qi, nt, pf: (h, 0, qi)),
            pl.BlockSpec((None, nk, t, 2 * HEAD_DIM), lambda h, qi, nt, pf: (h // group, 0, 0, 0)),
            pl.BlockSpec((None, nk, HEAD_DIM, t), lambda h, qi, nt, pf: (h // group, 0, 0, 0)),
            pl.BlockSpec((None, npad, 2 * HEAD_DIM), lambda h, qi, nt, pf: (h // group, 0, 0)),
            pl.BlockSpec((None, HEAD_DIM, npad), lambda h, qi, nt, pf: (h // group, 0, 0)),
        ],
        out_specs=pl.BlockSpec((t, HEAD_DIM), lambda h, qi, nt, pf: (qi, h)),
        scratch_shapes=[pltpu.VMEM((t, t), F32), pltpu.VMEM((t, t), F32)],
    )
    return pl.pallas_call(
        kern,
        grid_spec=grid_spec,
        out_shape=jax.ShapeDtypeStruct((m, n_heads * HEAD_DIM), F32),
        compiler_params=_cparams(("arbitrary", "arbitrary")),
        name="fox_attn",
    )(ntab, ptab, qat, ka, vt, kam, vtm)


def _tile_schedule(stats, k2_prefix_max, c2_prefix_last, *, n_kv, group):
    n_heads = n_kv * group
    nt = stats.shape[0]
    qmax = jnp.sqrt(stats[:, 0, :n_heads])
    kmax = jnp.sqrt(jnp.maximum(jnp.max(stats[:, 1, :n_kv], axis=0), k2_prefix_max))
    smax = qmax * jnp.repeat(kmax, group)[None, :]
    c_last = stats[:, 2, :n_heads]
    c_first = stats[:, 3, :n_heads]
    bias_max = c_first[:, None, :] - c_last[None, :, :]
    needed = 2.0 * smax[:, None, :] + bias_max > -SKIP_LOG2
    qi = jnp.arange(nt)[:, None, None]
    kt = jnp.arange(nt)[None, :, None]
    dist = jnp.where(jnp.logical_and(needed, kt < qi), qi - kt, 0)
    n_older = jnp.max(dist, axis=1)
    prefix_needed = 2.0 * smax + (c_first - c2_prefix_last[None, :]) > -SKIP_LOG2
    flat = lambda a: a.T.astype(jnp.int32).reshape(-1)
    return flat(n_older), flat(prefix_needed)


def _attn_small_kernel(q_ref, k_ref, v_ref, lf_ref, ck_ref, cv_ref, clf_ref, o_ref, c_ref, ccum_s,
                       *, n_kv, group, n_cached_seqs, p_len, s_len):
    b = pl.program_id(0)
    has_cache = b < n_cached_seqs
    n_blk = p_len // LANES
    tri = _lower_tri(LANES)

    def cblock(j, off):
        r0 = pl.multiple_of(j * LANES, LANES)
        cb = _split3_dot(tri, clf_ref[pl.ds(r0, LANES), :]) + off
        ccum_s[pl.ds(r0, LANES), :] = cb
        return cb[LANES - 1:LANES, :]

    total = lax.fori_loop(0, n_blk, cblock, jnp.zeros((1, LANES), F32))
    total = jnp.where(has_cache, total, 0.0)
    zrows = jnp.zeros((LANES - s_len, LANES), F32)
    c_pad = _split3_dot(tri, jnp.concatenate([lf_ref[...], zrows], axis=0)) + total
    c_own = c_pad[:s_len, :]
    c_ref[...] = c_own
    c_own_t = c_pad.T
    c_cache_t = ccum_s[...].T

    rows = lax.broadcasted_iota(jnp.int32, (group * s_len, LANES), 0) % s_len
    cols = lax.broadcasted_iota(jnp.int32, (group * s_len, LANES), 1)
    scale = HEAD_DIM ** -0.5
    for g in range(n_kv):
        ksl = slice(g * HEAD_DIM, (g + 1) * HEAD_DIM)
        k_own = jnp.concatenate([k_ref[:, ksl], zrows], axis=0).astype(BF16)
        v_own = jnp.concatenate([v_ref[:, ksl], zrows], axis=0).astype(BF16)
        k_cache = ck_ref[:, ksl].astype(BF16)
        v_cache = cv_ref[:, ksl].astype(BF16)
        heads = range(g * group, (g + 1) * group)
        stack = lambda f: jnp.concatenate([f(h) for h in heads], axis=0)
        q = stack(lambda h: (q_ref[:, h * HEAD_DIM:(h + 1) * HEAD_DIM] * scale).astype(BF16))
        s_own = _qk(q, k_own) + stack(lambda h: c_own[:, h:h + 1] - c_own_t[h:h + 1, :])
        s_own = jnp.where(cols <= rows, s_own, NEG_BIG)
        s_cache = _qk(q, k_cache) + stack(lambda h: c_own[:, h:h + 1] - c_cache_t[h:h + 1, :])
        s_cache = jnp.where(has_cache, s_cache, NEG_BIG)
        mx = jnp.maximum(jnp.max(s_own, axis=1, keepdims=True), jnp.max(s_cache, axis=1, keepdims=True))
        p_own = jnp.exp(s_own - mx)
        p_cache = jnp.exp(s_cache - mx)
        den = jnp.sum(p_own, axis=1, keepdims=True) + jnp.sum(p_cache, axis=1, keepdims=True)
        num = jnp.dot(p_own.astype(BF16), v_own, preferred_element_type=F32)
        num = num + jnp.dot(p_cache.astype(BF16), v_cache, preferred_element_type=F32)
        out = num / den
        for r, h in enumerate(heads):
            o_ref[:, h * HEAD_DIM:(h + 1) * HEAD_DIM] = out[r * s_len:(r + 1) * s_len, :]


def _attn_small(u, lf, cache_k, cache_v, cache_lf, *, n_seq, s_len, col_q, d_attn, n_kv, group):
    n_cached, p_len, dkv = cache_k.shape
    ck_blk, cv_blk = (col_q + d_attn) // dkv, (col_q + d_attn + dkv) // dkv
    cq_blk = col_q // d_attn
    cmap = lambda b: (jnp.minimum(b, n_cached - 1), 0, 0)
    kern = functools.partial(_attn_small_kernel, n_kv=n_kv, group=group, n_cached_seqs=n_cached, p_len=p_len,
                             s_len=s_len)
    return pl.pallas_call(
        kern,
        grid=(n_seq,),
        in_specs=[
            pl.BlockSpec((s_len, d_attn), lambda b: (b, cq_blk)),
            pl.BlockSpec((s_len, dkv), lambda b: (b, ck_blk)),
            pl.BlockSpec((s_len, dkv), lambda b: (b, cv_blk)),
            pl.BlockSpec((s_len, LANES), lambda b: (b, 0)),
            pl.BlockSpec((None, p_len, dkv), cmap),
            pl.BlockSpec((None, p_len, dkv), cmap),
            pl.BlockSpec((None, p_len, LANES), cmap),
        ],
        out_specs=[pl.BlockSpec((s_len, d_attn), lambda b: (b, 0)),
                   pl.BlockSpec((s_len, LANES), lambda b: (b, 0))],
        out_shape=[jax.ShapeDtypeStruct((n_seq * s_len, d_attn), F32),
                   jax.ShapeDtypeStruct((n_seq * s_len, LANES), F32)],
        scratch_shapes=[pltpu.VMEM((p_len, LANES), F32)],
        compiler_params=_cparams(("arbitrary",)),
        name="fox_attn_small",
    )(u, u, u, lf, cache_k, cache_v, cache_lf)


def _outproj_kernel(orn_ref, oat_ref, ga_ref, w_ref, x_ref, h_ref, a_s, *, d_rnn):
    def project(attn_half):
        part = x_ref[...] + jnp.dot(orn_ref[...], w_ref[:d_rnn, :], preferred_element_type=F32)
        h_ref[...] = part + jnp.dot(attn_half, w_ref[d_rnn:, :], preferred_element_type=F32)

    @pl.when(pl.program_id(1) == 0)
    def _():
        an = _rms(oat_ref[...], ga_ref[...]).astype(BF16)
        a_s[...] = an
        project(an)

    @pl.when(pl.program_id(1) > 0)
    def _():
        project(a_s[...])


def _outproj(orn, oat, ga, w, x, *, tm, tn):
    m, d_rnn = orn.shape
    d_attn = oat.shape[1]
    d = x.shape[1]
    return pl.pallas_call(
        functools.partial(_outproj_kernel, d_rnn=d_rnn),
        grid=(m // tm, d // tn),
        in_specs=[
            pl.BlockSpec((tm, d_rnn), lambda i, j: (i, 0)),
            pl.BlockSpec((tm, d_attn), lambda i, j: (i, 0)),
            pl.BlockSpec((1, d_attn), lambda i, j: (0, 0)),
            pl.BlockSpec((d_rnn + d_attn, tn), lambda i, j: (0, j)),
            pl.BlockSpec((tm, tn), lambda i, j: (i, j)),
        ],
        out_specs=pl.BlockSpec((tm, tn), lambda i, j: (i, j)),
        out_shape=jax.ShapeDtypeStruct((m, d), F32),
        scratch_shapes=[pltpu.VMEM((tm, d_attn), BF16)],
        compiler_params=_cparams(("arbitrary", "arbitrary")),
        name="outproj",
    )(orn, oat, ga, w, x)


def _norm_cast_kernel(h_ref, g_ref, o_ref):
    o_ref[...] = _rms(h_ref[...], g_ref[...]).astype(BF16)


def _norm_cast(h, g, *, tm):
    m, d = h.shape
    return pl.pallas_call(
        _norm_cast_kernel,
        grid=(m // tm,),
        in_specs=[pl.BlockSpec((tm, d), lambda i: (i, 0)), pl.BlockSpec((1, d), lambda i: (0, 0))],
        out_specs=pl.BlockSpec((tm, d), lambda i: (i, 0)),
        out_shape=jax.ShapeDtypeStruct((m, d), BF16),
        compiler_params=_cparams(("arbitrary",)),
        name="norm_cast",
    )(h, g)


def _ffn_in_kernel(xn_ref, wg_ref, wv_ref, wc_ref, bc_ref, first_ref, p1_ref, p2_ref, act_ref, zt_ref,
                   carry_s, *, tm, seq_len, multi_seq, n_sub):
    i = pl.program_id(0)
    f = pl.program_id(1)

    @pl.when(i == 0)
    def _():
        carry_s[f] = first_ref[...]
    before = carry_s[f]

    assert n_sub == 1 or not multi_seq
    rs = tm // n_sub
    for r0 in range(0, tm, rs):
        xn = xn_ref[r0:r0 + rs, :]
        zg = jnp.dot(xn, wg_ref[...], preferred_element_type=F32)
        zv = jnp.dot(xn, wv_ref[...], preferred_element_type=F32)
        prev1, prev2 = _prev_rows(zg, before, (p1_ref, p2_ref), seq_len=seq_len, multi_seq=multi_seq)
        gate = bc_ref[...] + wc_ref[0:1, :] * prev2
        gate = gate + wc_ref[1:2, :] * prev1
        gate = gate + wc_ref[2:3, :] * zg
        before = zg[rs - SUBLANES:, :]
        if multi_seq:
            zt_ref[...] = zg
        act_ref[r0:r0 + rs, :] = (_gelu(gate) * zv).astype(BF16)
    if not multi_seq:
        carry_s[f] = before
        zt_ref[...] = before


def _ffn_in(xn, w, wc, bc, ffn_state, *, tm, tf, seq_len):
    m, d = xn.shape
    d_ff = w.shape[1] // 2
    nf = d_ff // tf
    multi_seq = m // seq_len > 1
    first, states = _conv_state_inputs(ffn_state, rows=tm, seq_len=seq_len, multi_seq=multi_seq)
    zt_rows = tm if multi_seq else SUBLANES
    st_spec = lambda a: pl.BlockSpec((a.shape[0], tf), lambda i, f: (0, f))
    sub_rows = 2 * LANES
    return pl.pallas_call(
        functools.partial(_ffn_in_kernel, tm=tm, seq_len=seq_len, multi_seq=multi_seq,
                          n_sub=tm // sub_rows if tm % sub_rows == 0 else 1),
        grid=(m // tm, nf),
        in_specs=[
            pl.BlockSpec((tm, d), lambda i, f: (i, 0)),
            pl.BlockSpec((d, tf), lambda i, f: (0, f)),
            pl.BlockSpec((d, tf), lambda i, f: (0, f + nf)),
            pl.BlockSpec((wc.shape[0], tf), lambda i, f: (0, f)),
            pl.BlockSpec((1, tf), lambda i, f: (0, f)),
            st_spec(first)] + [st_spec(s) for s in states],
        out_specs=[pl.BlockSpec((tm, tf), lambda i, f: (i, f)),
                   pl.BlockSpec((zt_rows, tf), lambda i, f: (i, f))],
        out_shape=[jax.ShapeDtypeStruct((m, d_ff), BF16),
                   jax.ShapeDtypeStruct((m // tm * zt_rows, d_ff), F32)],
        scratch_shapes=[pltpu.VMEM((nf, SUBLANES, tf), F32)],
        compiler_params=_cparams(("arbitrary", "arbitrary")),
        name="ffn_in",
    )(xn, w, w, wc, bc, first, *states)


def _ffn_out_kernel(act_ref, w_ref, h_ref, g_ref, y_ref, *, tc):
    k = pl.program_id(1)
    d = y_ref.shape[1]
    chunks = [slice(c, c + tc) for c in range(0, d, tc)]

    @pl.when(k == 0)
    def _():
        y_ref[...] = h_ref[...]

    act = act_ref[...]
    for sl in chunks:
        y_ref[:, sl] += jnp.dot(act, w_ref[:, sl], preferred_element_type=F32)

    @pl.when(k == pl.num_programs(1) - 1)
    def _():
        ssq = jnp.zeros((y_ref.shape[0], 1), F32)
        for sl in chunks:
            yc = y_ref[:, sl]
            ssq = ssq + jnp.sum(yc * yc, axis=-1, keepdims=True)
        inv = lax.rsqrt(ssq / d + EPS)
        for sl in chunks:
            y_ref[:, sl] = y_ref[:, sl] * inv * g_ref[:, sl]


def _ffn_out(act, w, h, g, *, tm, tk):
    m, d_ff = act.shape
    d = h.shape[1]
    return pl.pallas_call(
        functools.partial(_ffn_out_kernel, tc=_pick(d, 1024)),
        grid=(m // tm, d_ff // tk),
        in_specs=[
            pl.BlockSpec((tm, tk), lambda i, k: (i, k)),
            pl.BlockSpec((tk, d), lambda i, k: (k, 0)),
            pl.BlockSpec((tm, d), lambda i, k: (i, 0)),
            pl.BlockSpec((1, d), lambda i, k: (0, 0)),
        ],
        out_specs=pl.BlockSpec((tm, d), lambda i, k: (i, 0)),
        out_shape=jax.ShapeDtypeStruct((m, d), F32),
        compiler_params=_cparams(("arbitrary", "arbitrary")),
        name="ffn_out",
    )(act, w, h, g)


def _pick(n, pref):
    t = min(n, pref)
    while n % t:
        t //= 2
    return t


def _place_state(state, seq_len, back):
    n_seq, w, c = state.shape
    out = jnp.zeros((n_seq, seq_len, c), state.dtype)
    out = out.at[:, :back, :].set(state[:, w - back:, :])
    return out.reshape(n_seq * seq_len, c)


def kernel(x_prompt, x_sample, cache_k, cache_v, cache_logf, state_rnn_h, state_rnn_conv, state_ffn_conv,
           meta_tokens, g_mix, w_in, b_f, w_rnn_conv, b_rnn_conv, w_rg_a, b_rg_a, w_rg_x, b_rg_x, lru_lambda,
           g_out_rnn, g_out_attn, w_out, g_ffn, w_ffn_in, w_ffn_conv, b_ffn_conv, w_ffn_out, g_final):
    depth = g_mix.shape[0]
    assert depth == 1, "single-layer stack"
    batch, seq, d_model = x_prompt.shape
    assert batch == 1
    dec_batch, dec_seq, _ = x_sample.shape
    n_meta = meta_tokens.shape[0]
    assert n_meta == dec_seq, "meta prefix is run as one more sample-length sequence"
    n_heads = b_f.shape[1]
    n_kv = cache_k.shape[3]
    group = n_heads // n_kv
    assert group <= SUBLANES and cache_k.shape[4] == HEAD_DIM
    d_attn = n_heads * HEAD_DIM
    d_rnn = state_rnn_h.shape[2]
    dkv = n_kv * HEAD_DIM
    d_ff = state_ffn_conv.shape[3]
    past = cache_k.shape[2]
    col_q = 2 * d_rnn
    n_main_cols = col_q + d_attn + 2 * dkv
    n_rnn_blocks = w_rg_a.shape[1]
    rnn_w = w_rnn_conv.shape[1]
    ffn_w = w_ffn_conv.shape[1]
    assert rnn_w == 4 and ffn_w == 3

    w_in_b = w_in[0].astype(BF16)
    wf_b = jnp.pad(w_in[0][:, n_main_cols:], ((0, 0), (0, LANES - n_heads))).astype(BF16)
    bf_p = jnp.pad(b_f[0], (0, LANES - n_heads)).reshape(1, LANES)
    w_out_b = w_out[0].astype(BF16)
    w_ffn_in_b = w_ffn_in[0].astype(BF16)
    w_ffn_out_b = w_ffn_out[0].astype(BF16)
    wg = jnp.concatenate([w_rg_a[0], w_rg_x[0]], axis=-1).astype(BF16)
    row = lambda a: a.reshape(1, -1)
    rg_args = (w_rnn_conv[0], row(b_rnn_conv[0]), wg, row(b_rg_a[0]), row(b_rg_x[0]), row(lru_lambda[0]),
               row(g_out_rnn[0]))

    def chain(x, seq_len, rnn_state, h0, ffn_state, attn_fn, tm, tm_ffn, tr, tn, tf, tk):
        u, lf = _inproj(x, row(g_mix[0]), w_in_b, wf_b, bf_p, nc=n_main_cols, tm=tm, tn=tn)
        orn, h_last = _rglru(u, rnn_state, h0, *rg_args, tr=tr, seq_len=seq_len, d_rnn=d_rnn)
        oat, attn_aux = attn_fn(u, lf)
        h1 = _outproj(orn, oat, row(g_out_attn[0]), w_out_b, x, tm=tm, tn=_pick(d_model, tn))
        xn = _norm_cast(h1, row(g_ffn[0]), tm=tm)
        act, zt = _ffn_in(xn, w_ffn_in_b, w_ffn_conv[0], row(b_ffn_conv[0]), ffn_state,
                          tm=tm_ffn, tf=tf, seq_len=seq_len)
        y = _ffn_out(act, w_ffn_out_b, h1, row(g_final), tm=tm, tk=tk)
        return u, lf, h_last, zt, y, attn_aux

    n_small = dec_batch + 1
    ms = n_small * dec_seq
    xs = jnp.concatenate([x_sample.reshape(dec_batch * dec_seq, d_model), meta_tokens.astype(F32)], axis=0)
    zpad = lambda a: jnp.concatenate([a, jnp.zeros((1,) + a.shape[1:], a.dtype)], axis=0)
    clf = jnp.pad(cache_logf[0], ((0, 0), (0, 0), (0, LANES - n_heads)))

    def small_attn(u, lf):
        return _attn_small(u, lf, cache_k[0].reshape(dec_batch, past, dkv), cache_v[0].reshape(dec_batch, past, dkv),
                           clf, n_seq=n_small, s_len=dec_seq, col_q=col_q, d_attn=d_attn, n_kv=n_kv, group=group)

    tn, tf, tk = _pick(n_main_cols, 1024), _pick(d_ff, 512), _pick(d_ff, 1024)
    u_s, lf_s, hl_s, zt_s, y_s, c_s = chain(
        xs, dec_seq, zpad(state_rnn_conv[0]), zpad(state_rnn_h[0]), zpad(state_ffn_conv[0]), small_attn,
        tm=ms, tm_ffn=ms, tr=ms, tn=tn, tf=tf, tk=tk)

    m0 = dec_batch * dec_seq
    u_meta = u_s[m0:]
    k_meta = u_meta[:, col_q + d_attn:col_q + d_attn + dkv]
    v_meta = u_meta[:, col_q + d_attn + dkv:]
    c_meta = c_s[m0:]
    npad = -(-n_meta // 16) * 16
    c2m = (c_meta[:, :n_heads] * LOG2E).reshape(n_meta, n_kv, group).transpose(1, 0, 2)
    c2m = jnp.pad(c2m, ((0, 0), (0, npad - n_meta), (0, 0)), constant_values=-NEG_BIG)
    bias_cols = jnp.stack(_bf16_split3(c2m), axis=-1).reshape(n_kv, npad, 3 * group)
    bias_cols = jnp.concatenate([bias_cols, jnp.ones((n_kv, npad, 3), BF16),
                                 jnp.zeros((n_kv, npad, HEAD_DIM - 3 * group - 3), BF16)], axis=-1)
    k_meta_p = jnp.pad(k_meta, ((0, npad - n_meta), (0, 0))).reshape(npad, n_kv, HEAD_DIM).transpose(1, 0, 2)
    kam = jnp.concatenate([k_meta_p.astype(BF16), bias_cols], axis=-1)
    vtm = jnp.pad(v_meta, ((0, npad - n_meta), (0, 0))).reshape(npad, n_kv, HEAD_DIM).transpose(1, 2, 0)
    vtm = vtm.astype(BF16)
    tq = _pick(seq, 512)

    def main_attn(u, lf):
        qat, ka, vt, stats = _prep2(u, lf, c_meta[n_meta - 1:], tm=tq, col_q=col_q, d_attn=d_attn, n_kv=n_kv,
                                    group=group)
        k_meta_f = k_meta.astype(BF16).astype(F32).reshape(n_meta, n_kv, HEAD_DIM)
        ntab, ptab = _tile_schedule(stats, jnp.max(jnp.sum(k_meta_f * k_meta_f, axis=-1), axis=0),
                                    c_meta[n_meta - 1, :n_heads] * LOG2E, n_kv=n_kv, group=group)
        return _attn2(ntab, ptab, qat, ka, vt, kam, vtm, t=tq, group=group), None

    u_m, lf_m, hl_m, zt_m, y_m, _ = chain(
        x_prompt[0], seq, u_meta[None, n_meta - (rnn_w - 1):, :d_rnn], hl_s[dec_batch:],
        zt_s[None, ms - (ffn_w - 1):, :], main_attn,
        tm=_pick(seq, 512), tm_ffn=_pick(seq, 1024), tr=_pick(seq, 256), tn=tn, tf=tf, tk=tk)

    kcols = slice(col_q + d_attn, col_q + d_attn + dkv)
    vcols = slice(col_q + d_attn + dkv, n_main_cols)
    y_prompt = y_m[None]
    y_sample = y_s[:m0].reshape(dec_batch, dec_seq, d_model)
    new_k_prompt = jnp.concatenate([k_meta, u_m[:, kcols]], axis=0).reshape(1, 1, n_meta + seq, n_kv, HEAD_DIM)
    new_v_prompt = jnp.concatenate([v_meta, u_m[:, vcols]], axis=0).reshape(1, 1, n_meta + seq, n_kv, HEAD_DIM)
    new_logf_prompt = jnp.concatenate([lf_s[m0:, :n_heads], lf_m[:, :n_heads]], axis=0)[None, None]
    new_rnn_h_prompt = hl_m[None]
    new_rnn_conv_prompt = u_m[seq - (rnn_w - 1):, :d_rnn][None, None]
    new_ffn_conv_prompt = zt_m[zt_m.shape[0] - (ffn_w - 1):][None, None]
    us3 = u_s[:m0].reshape(dec_batch, dec_seq, n_main_cols)
    new_k_sample = us3[:, :, kcols].reshape(1, dec_batch, dec_seq, n_kv, HEAD_DIM)
    new_v_sample = us3[:, :, vcols].reshape(1, dec_batch, dec_seq, n_kv, HEAD_DIM)
    new_logf_sample = lf_s[:m0, :n_heads].reshape(1, dec_batch, dec_seq, n_heads)
    new_rnn_h_sample = hl_s[:dec_batch][None]
    new_rnn_conv_sample = us3[:, dec_seq - (rnn_w - 1):, :d_rnn][None]
    new_ffn_conv_sample = zt_s[:m0].reshape(dec_batch, dec_seq, d_ff)[:, dec_seq - (ffn_w - 1):][None]
    return (y_prompt, y_sample, new_k_prompt, new_v_prompt, new_logf_prompt, new_rnn_h_prompt,
            new_rnn_conv_prompt, new_ffn_conv_prompt, new_k_sample, new_v_sample, new_logf_sample,
            new_rnn_h_sample, new_rnn_conv_sample, new_ffn_conv_sample)
```
